```python
import math
import jax, jax.numpy as jnp
from jax import lax
import numpy as np

D_MODEL = 1024
BATCH = 16
SEQ = 2048
DEPTH = 2

N_A_LAYERS = DEPTH // 2
N_B_LAYERS = DEPTH - N_A_LAYERS
N_DENSE_LAYERS = (DEPTH + 1) // 2
N_MOE_LAYERS = DEPTH // 2

N_META = 16

RG_BLOCK_WIDTH = 128
D_RNN = (4 * D_MODEL // 3) // RG_BLOCK_WIDTH * RG_BLOCK_WIDTH
N_RG_BLOCKS = D_RNN // RG_BLOCK_WIDTH
CONV_WIDTH = 4
RG_C = 8.0

N_HEADS = D_MODEL // 64
QK_NOPE = 64
QK_ROPE = 32
V_HEAD = 64
KV_LORA = D_MODEL // 4
Q_LORA = 3 * D_MODEL // 8
ROPE_THETA = 10000.0
Q_BLOCK = 128

D_FF = 3 * D_MODEL
N_EXPERTS = 8
TOP_K = 2
E_FF = 7 * D_MODEL // 2

RMS_EPS = 1e-6

kernel_name = "yoco_rglru_mla_moe_hybrid"


def rms_norm(x, gain):
    xf = x.astype(jnp.float32)
    y = xf * lax.rsqrt(jnp.mean(xf * xf, axis=-1, keepdims=True) + RMS_EPS) * gain.astype(jnp.float32)
    return y.astype(x.dtype)


def rope_tables(t_len):
    pos = jnp.arange(t_len, dtype=jnp.float32)
    inv = ROPE_THETA ** (-jnp.arange(0, QK_ROPE, 2, dtype=jnp.float32) / QK_ROPE)
    ang = pos[:, None] * inv[None, :]
    return jnp.cos(ang), jnp.sin(ang)


def apply_rope(x, cos, sin):
    xf = x.astype(jnp.float32)
    x1, x2 = jnp.split(xf, 2, axis=-1)
    out = jnp.concatenate([x1 * cos - x2 * sin, x2 * cos + x1 * sin], axis=-1)
    return out.astype(x.dtype)


def _lru_combine(e1, e2):
    a1, b1 = e1
    a2, b2 = e2
    return a1 * a2, a2 * b1 + b2


def rglru_block(u, w_in, conv_w, conv_b, ga_w, ga_b, gx_w, gx_b, lam, w_out):
    b, t, _ = u.shape
    y, xr = jnp.split(u @ w_in, 2, axis=-1)
    xp = jnp.pad(xr, ((0, 0), (CONV_WIDTH - 1, 0), (0, 0)))
    xc = conv_b + sum(xp[:, k:k + t] * conv_w[k] for k in range(CONV_WIDTH))
    xb = xc.reshape(b, t, N_RG_BLOCKS, RG_BLOCK_WIDTH)
    r = jax.nn.sigmoid(jnp.einsum('btnc,ncd->btnd', xb, ga_w).reshape(b, t, D_RNN) + ga_b)
    ig = jax.nn.sigmoid(jnp.einsum('btnc,ncd->btnd', xb, gx_w).reshape(b, t, D_RNN) + gx_b)
    log_a = -RG_C * r.astype(jnp.float32) * jax.nn.softplus(-lam.astype(jnp.float32))
    a = jnp.exp(log_a)
    bx = jnp.sqrt(-jnp.expm1(2.0 * log_a)) * (ig * xc).astype(jnp.float32)
    _, hs = lax.associative_scan(_lru_combine, (a, bx), axis=1)
    return (jax.nn.gelu(y) * hs.astype(u.dtype)) @ w_out


def shared_latent_kv(h, kv_norm, w_dkv, latent_norm, w_ukv, cos, sin):
    b, t, _ = h.shape
    u = rms_norm(h, kv_norm)
    ckv, k_rope = jnp.split(u @ w_dkv, [KV_LORA], axis=-1)
    ckv = rms_norm(ckv, latent_norm)
    kv = (ckv @ w_ukv).reshape(b, t, N_HEADS, QK_NOPE + V_HEAD)
    k_nope, v = jnp.split(kv, [QK_NOPE], axis=-1)
    k_rope = apply_rope(k_rope, cos, sin)
    return k_nope, k_rope, v


def mla_block(u, k_nope, k_rope, v, w_dq, q_norm, w_uq, w_o, cos, sin):
    b, t, _ = u.shape
    cq = rms_norm(u @ w_dq, q_norm)
    q = (cq @ w_uq).reshape(b, t, N_HEADS, QK_NOPE + QK_ROPE)
    q_nope, q_rope = jnp.split(q, [QK_NOPE], axis=-1)
    q_rope = apply_rope(q_rope, cos[:, None, :], sin[:, None, :])
    n_blk = -(-t // Q_BLOCK)
    pad = n_blk * Q_BLOCK - t
    padt = lambda z: jnp.pad(z, [(0, 0), (0, pad)] + [(0, 0)] * (z.ndim - 2))
    q_nope, q_rope, kn, kr, vv = padt(q_nope), padt(q_rope), padt(k_nope), padt(k_rope), padt(v)
    scale = (QK_NOPE + QK_ROPE) ** -0.5
    neg = jnp.finfo(jnp.float32).min
    outs = []
    for blk in range(n_blk):
        s0, e = blk * Q_BLOCK, (blk + 1) * Q_BLOCK
        s = (jnp.einsum('bqhd,bkhd->bhqk', q_nope[:, s0:e], kn[:, :e])
             + jnp.einsum('bqhr,bkr->bhqk', q_rope[:, s0:e], kr[:, :e])).astype(jnp.float32) * scale
        mask = jnp.arange(s0, e)[:, None] >= jnp.arange(e)[None, :]
        p = jax.nn.softmax(jnp.where(mask, s, neg), axis=-1).astype(vv.dtype)
        outs.append(jnp.einsum('bhqk,bkhd->bqhd', p, vv[:, :e]))
    o = jnp.concatenate(outs, axis=1)[:, :t].reshape(b, t, N_HEADS * V_HEAD)
    return o @ w_o


def swiglu(u, w_gate_up, w_down):
    g, up = jnp.split(u @ w_gate_up, 2, axis=-1)
    return (jax.nn.silu(g) * up) @ w_down


def moe_swiglu(u, router_w, router_b, w_gate_up, w_down):
    b, t, d = u.shape
    tok = u.reshape(b * t, d)
    logits = (tok @ router_w).astype(jnp.float32) + router_b.astype(jnp.float32)
    probs = jax.nn.softmax(logits, axis=-1)
    top_p, top_i = lax.top_k(probs, TOP_K)
    top_p = top_p / jnp.sum(top_p, axis=-1, keepdims=True)
    gate = jnp.sum(jax.nn.one_hot(top_i, N_EXPERTS, dtype=jnp.float32) * top_p[..., None], axis=1)
    y = jnp.zeros_like(tok)
    for ex in range(N_EXPERTS):
        y = y + gate[:, ex:ex + 1].astype(tok.dtype) * swiglu(tok, w_gate_up[ex], w_down[ex])
    return y.reshape(b, t, d)


def setup_inputs(seed: int = 0) -> dict:
    key = jax.random.key(seed)
    ks = iter(jax.random.split(key, 48))
    nrm = lambda shape, fan_in: jax.random.normal(next(ks), shape, jnp.float32) * (fan_in ** -0.5)
    gain = lambda shape: 1.0 + 0.02 * jax.random.normal(next(ks), shape, jnp.float32)
    small = lambda shape, s=0.02: s * jax.random.normal(next(ks), shape, jnp.float32)
    na, nb, nd, nm = N_A_LAYERS, N_B_LAYERS, N_DENSE_LAYERS, N_MOE_LAYERS
    a_c = jax.random.uniform(next(ks), (na, D_RNN), jnp.float32, 0.9, 0.999)
    a_base = a_c ** (1.0 / RG_C)
    lam = jnp.log(a_base) - jnp.log1p(-a_base)
    return {
        "x": jax.random.normal(next(ks), (BATCH, SEQ, D_MODEL), jnp.float32),
        "meta_tokens": jax.random.normal(next(ks), (N_META, D_MODEL), jnp.float32),
        "a_pre_norm": gain((na, D_MODEL)),
        "a_w_in": nrm((na, D_MODEL, 2 * D_RNN), D_MODEL),
        "a_conv_w": nrm((na, CONV_WIDTH, D_RNN), CONV_WIDTH),
        "a_conv_b": small((na, D_RNN)),
        "a_gate_a_w": nrm((na, N_RG_BLOCKS, RG_BLOCK_WIDTH, RG_BLOCK_WIDTH), RG_BLOCK_WIDTH),
        "a_gate_a_b": small((na, D_RNN)),
        "a_gate_x_w": nrm((na, N_RG_BLOCKS, RG_BLOCK_WIDTH, RG_BLOCK_WIDTH), RG_BLOCK_WIDTH),
        "a_gate_x_b": small((na, D_RNN)),
        "a_lambda": lam,
        "a_w_out": nrm((na, D_RNN, D_MODEL), D_RNN),
        "a_post_norm": gain((na, D_MODEL)),
        "kv_norm": gain((D_MODEL,)),
        "kv_w_dkv": nrm((D_MODEL, KV_LORA + QK_ROPE), D_MODEL),
        "kv_latent_norm": gain((KV_LORA,)),
        "kv_w_ukv": nrm((KV_LORA, N_HEADS * (QK_NOPE + V_HEAD)), KV_LORA),
        "b_pre_norm": gain((nb, D_MODEL)),
        "b_w_dq": nrm((nb, D_MODEL, Q_LORA), D_MODEL),
        "b_q_norm": gain((nb, Q_LORA)),
        "b_w_uq": nrm((nb, Q_LORA, N_HEADS * (QK_NOPE + QK_ROPE)), Q_LORA),
        "b_w_o": nrm((nb, N_HEADS * V_HEAD, D_MODEL), N_HEADS * V_HEAD),
        "b_post_norm": gain((nb, D_MODEL)),
        "f_pre_norm": gain((nd, D_MODEL)),
        "f_w_gate_up": nrm((nd, D_MODEL, 2 * D_FF), D_MODEL),
        "f_w_down": nrm((nd, D_FF, D_MODEL), D_FF),
        "f_post_norm": gain((nd, D_MODEL)),
        "m_pre_norm": gain((nm, D_MODEL)),
        "m_router_w": nrm((nm, D_MODEL, N_EXPERTS), D_MODEL),
        "m_router_b": small((nm, N_EXPERTS), 0.01),
        "m_w_gate_up": nrm((nm, N_EXPERTS, D_MODEL, 2 * E_FF), D_MODEL),
        "m_w_down": nrm((nm, N_EXPERTS, E_FF, D_MODEL), E_FF),
        "m_post_norm": gain((nm, D_MODEL)),
    }


def reference(x, meta_tokens,
              a_pre_norm, a_w_in, a_conv_w, a_conv_b, a_gate_a_w, a_gate_a_b, a_gate_x_w, a_gate_x_b,
              a_lambda, a_w_out, a_post_norm,
              kv_norm, kv_w_dkv, kv_latent_norm, kv_w_ukv,
              b_pre_norm, b_w_dq, b_q_norm, b_w_uq, b_w_o, b_post_norm,
              f_pre_norm, f_w_gate_up, f_w_down, f_post_norm,
              m_pre_norm, m_router_w, m_router_b, m_w_gate_up, m_w_down, m_post_norm):
    b = x.shape[0]
    h = jnp.concatenate([jnp.broadcast_to(meta_tokens.astype(x.dtype)[None], (b, N_META, x.shape[2])), x], axis=1)
    t = h.shape[1]
    cos, sin = rope_tables(t)
    k_nope = k_rope = v = None
    for layer in range(DEPTH):
        if layer < N_A_LAYERS:
            i = layer
            mix = rglru_block(rms_norm(h, a_pre_norm[i]), a_w_in[i], a_conv_w[i], a_conv_b[i],
                              a_gate_a_w[i], a_gate_a_b[i], a_gate_x_w[i], a_gate_x_b[i],
                              a_lambda[i], a_w_out[i])
            h = h + rms_norm(mix, a_post_norm[i])
        else:
            if layer == N_A_LAYERS:
                k_nope, k_rope, v = shared_latent_kv(h, kv_norm, kv_w_dkv, kv_latent_norm, kv_w_ukv, cos, sin)
            i = layer - N_A_LAYERS
            mix = mla_block(rms_norm(h, b_pre_norm[i]), k_nope, k_rope, v,
                            b_w_dq[i], b_q_norm[i], b_w_uq[i], b_w_o[i], cos, sin)
            h = h + rms_norm(mix, b_post_norm[i])
        j = layer // 2
        if layer % 2 == 0:
            ff = swiglu(rms_norm(h, f_pre_norm[j]), f_w_gate_up[j], f_w_down[j])
            h = h + rms_norm(ff, f_post_norm[j])
        else:
            ff = moe_swiglu(rms_norm(h, m_pre_norm[j]), m_router_w[j], m_router_b[j], m_w_gate_up[j], m_w_down[j])
            h = h + rms_norm(ff, m_post_norm[j])
    return h[:, N_META:]
```

```python
import functools

import jax
import jax.numpy as jnp
from jax import lax
from jax.experimental import pallas as pl
from jax.experimental.pallas import tpu as pltpu

F32 = jnp.float32
BF16 = jnp.bfloat16
I32 = jnp.int32

RMS_EPS = 1e-6
RG_C = 8.0
ROPE_THETA = 10000.0
QK_NOPE = 64
QK_ROPE = 32
V_HEAD = 64
HEAD_PAD = 128
TOP_K = 2
LANES = 128

TT = 48
TT_SMALL = 24
ATT_BLK = 256
MOE_TILE = 512
VMEM_LIMIT = 56 * 1024 * 1024


def _cparams(sem):
    return pltpu.CompilerParams(dimension_semantics=sem, vmem_limit_bytes=VMEM_LIMIT)


def _rms(x, g):
    ms = jnp.mean(x * x, axis=-1, keepdims=True)
    return x * lax.rsqrt(ms + RMS_EPS) * g


def _dot(a, b):
    return jnp.dot(a, b, preferred_element_type=F32)


def _dot_t(a, b):
    return lax.dot_general(a, b, (((1,), (1,)), ((), ())), preferred_element_type=F32)


def _a1_body(h_ref, g_ref, w_ref, y_ref, xr_ref, *, c):
    xn = _rms(h_ref[...], g_ref[...]).astype(BF16)
    y_ref[...] = _dot(xn, w_ref[:, :c]).astype(BF16)
    xr_ref[...] = _dot(xn, w_ref[:, c:]).astype(BF16)


def _a1(h0, gain, w_in, *, tm):
    r, d = h0.shape
    c = w_in.shape[1] // 2
    row = lambda i: (i, 0)
    fixed = lambda i: (0, 0)
    return pl.pallas_call(
        functools.partial(_a1_body, c=c),
        grid=(r // tm,),
        in_specs=[pl.BlockSpec((tm, d), row), pl.BlockSpec((1, d), fixed),
                  pl.BlockSpec((d, 2 * c), fixed)],
        out_specs=[pl.BlockSpec((tm, c), row), pl.BlockSpec((tm, c), row)],
        out_shape=[jax.ShapeDtypeStruct((r, c), BF16), jax.ShapeDtypeStruct((r, c), BF16)],
        compiler_params=_cparams(("arbitrary",)),
        name="a1_norm_win",
    )(h0, gain, w_in)


def _a2_body(xr_ref, y_ref, h_ref, cw_ref, cb_ref, wg_ref, gab_ref, gxb_ref, lam_ref,
             wo_ref, pn_ref, out_ref, ext_s, a_s, b_s, carry_s, *, nb, tt, bsz, cw):
    tm = tt * bsz
    tail = (cw - 1) * bsz

    @pl.when(pl.program_id(0) == 0)
    def _():
        ext_s[0:tail, :] = jnp.zeros((tail, ext_s.shape[1]), F32)
        carry_s[...] = jnp.zeros_like(carry_s)

    ext_s[tail:tail + tm, :] = xr_ref[...].astype(F32)
    xc = cb_ref[...] + ext_s[0:tm, :] * cw_ref[0:1, :]
    for k in range(1, cw):
        xc = xc + ext_s[k * bsz:k * bsz + tm, :] * cw_ref[k:k + 1, :]
    ext_s[0:tail, :] = ext_s[tm:tm + tail, :]

    neg_lam = -lam_ref[...]
    sp = jnp.maximum(neg_lam, 0.0) + jnp.log1p(jnp.exp(-jnp.abs(neg_lam)))
    xcb = xc.astype(BF16)
    for n in range(nb):
        sl = slice(n * LANES, (n + 1) * LANES)
        g = _dot(xcb[:, sl], wg_ref[n])
        r = jax.nn.sigmoid(g[:, :LANES] + gab_ref[:, sl])
        ig = jax.nn.sigmoid(g[:, LANES:] + gxb_ref[:, sl])
        log_a = (-RG_C) * r * sp[:, sl]
        a = jnp.exp(log_a)
        a_s[:, sl] = a
        b_s[:, sl] = jnp.sqrt(1.0 - a * a) * (ig * xc[:, sl])

    def step(t, h):
        rows = pl.ds(pl.multiple_of(t * bsz, bsz), bsz)
        h = a_s[rows, :] * h + b_s[rows, :]
        b_s[rows, :] = h
        return h

    carry_s[...] = lax.fori_loop(0, tt, step, carry_s[...], unroll=4)

    z = (jax.nn.gelu(y_ref[...].astype(F32)) * b_s[...]).astype(BF16)
    mix = _dot(z, wo_ref[...])
    out_ref[...] = h_ref[...] + _rms(mix, pn_ref[...])


def _a2(xr, y, h0, conv_w, conv_b, wg, ga_b, gx_b, lam, w_out, post, *, tt, bsz):
    r, c = xr.shape
    d = h0.shape[1]
    nb = wg.shape[0]
    cw = conv_w.shape[0]
    tm = tt * bsz
    row = lambda i: (i, 0)
    fixed2 = lambda i: (0, 0)
    fixed3 = lambda i: (0, 0, 0)
    return pl.pallas_call(
        functools.partial(_a2_body, nb=nb, tt=tt, bsz=bsz, cw=cw),
        grid=(r // tm,),
        in_specs=[pl.BlockSpec((tm, c), row), pl.BlockSpec((tm, c), row), pl.BlockSpec((tm, d), row),
                  pl.BlockSpec((cw, c), fixed2), pl.BlockSpec((1, c), fixed2),
                  pl.BlockSpec((nb, LANES, 2 * LANES), fixed3),
                  pl.BlockSpec((1, c), fixed2), pl.BlockSpec((1, c), fixed2), pl.BlockSpec((1, c), fixed2),
                  pl.BlockSpec((c, d), fixed2), pl.BlockSpec((1, d), fixed2)],
        out_specs=pl.BlockSpec((tm, d), row),
        out_shape=jax.ShapeDtypeStruct((r, d), F32),
        scratch_shapes=[pltpu.VMEM((tm + (cw - 1) * bsz, c), F32), pltpu.VMEM((tm, c), F32),
                        pltpu.VMEM((tm, c), F32), pltpu.VMEM((bsz, c), F32)],
        compiler_params=_cparams(("arbitrary",)),
        name="a2_rglru",
    )(xr, y, h0, conv_w, conv_b, wg, ga_b, gx_b, lam, w_out, post)


def _a3_body(h_ref, pre_ref, wg_ref, wu_ref, wd_ref, post_ref, out_ref, xn_s, acc_s):
    f = pl.program_id(1)

    @pl.when(f == 0)
    def _():
        xn_s[...] = _rms(h_ref[...], pre_ref[...]).astype(BF16)
        acc_s[...] = jnp.zeros_like(acc_s)

    xn = xn_s[...]
    g = _dot(xn, wg_ref[...])
    u = _dot(xn, wu_ref[...])
    acc_s[...] += _dot((jax.nn.silu(g) * u).astype(BF16), wd_ref[...])

    @pl.when(f == pl.num_programs(1) - 1)
    def _():
        out_ref[...] = h_ref[...] + _rms(acc_s[...], post_ref[...])


def _a3(h1, pre, w_gu, w_d, post, *, tm, tf):
    r, d = h1.shape
    ff = w_d.shape[0]
    nf = ff // tf
    row = lambda i, f: (i, 0)
    fixed = lambda i, f: (0, 0)
    return pl.pallas_call(
        _a3_body,
        grid=(r // tm, nf),
        in_specs=[pl.BlockSpec((tm, d), row), pl.BlockSpec((1, d), fixed),
                  pl.BlockSpec((d, tf), lambda i, f: (0, f)),
                  pl.BlockSpec((d, tf), lambda i, f: (0, nf + f)),
                  pl.BlockSpec((tf, d), lambda i, f: (f, 0)),
                  pl.BlockSpec((1, d), fixed)],
        out_specs=pl.BlockSpec((tm, d), row),
        out_shape=jax.ShapeDtypeStruct((r, d), F32),
        scratch_shapes=[pltpu.VMEM((tm, d), BF16), pltpu.VMEM((tm, d), F32)],
        compiler_params=_cparams(("arbitrary", "arbitrary")),
        name="a3_dense_ffn",
    )(h1, pre, w_gu, w_gu, w_d, post)


def _b1_body(h_ref, kvn_ref, bpre_ref, wc_ref, wka_ref, wkb_ref, latn_ref, wk_ref, wv_ref,
             wdq_ref, qn_ref, wq_ref, tq_ref, tkc_ref, tks_ref, q_ref, k_ref, v_ref, *, nh):
    h = h_ref[...]
    hn = h * lax.rsqrt(jnp.mean(h * h, axis=-1, keepdims=True) + RMS_EPS)
    ukv = (hn * kvn_ref[...]).astype(BF16)
    uq = (hn * bpre_ref[...]).astype(BF16)

    ckv = _rms(_dot(ukv, wc_ref[...]), latn_ref[...]).astype(BF16)
    kr = _dot(ukv, wka_ref[...]) * tkc_ref[...] + _dot(ukv, wkb_ref[...]) * tks_ref[...]
    kn = _dot(ckv, wk_ref[...])
    for hd in range(nh):
        sl = slice(hd * HEAD_PAD, (hd + 1) * HEAD_PAD)
        k_ref[:, sl] = (kn[:, sl] + kr).astype(BF16)
    v_ref[...] = _dot(ckv, wv_ref[...]).astype(BF16)

    cq = _rms(_dot(uq, wdq_ref[...]), qn_ref[...]).astype(BF16)
    qr = _dot(cq, wq_ref[...])
    tq = tq_ref[...]
    for hd in range(nh):
        sl = slice(hd * HEAD_PAD, (hd + 1) * HEAD_PAD)
        q_ref[:, sl] = (qr[:, sl] * tq).astype(BF16)


def _b1(h2, kvn, bpre, wc, wka, wkb, latn, wk, wv, wdq, qn, wq, tq, tkc, tks, *, tm, nh):
    r, d = h2.shape
    row = lambda i: (i, 0)
    fixed = lambda i: (0, 0)
    full = lambda a: pl.BlockSpec(a.shape, fixed)
    return pl.pallas_call(
        functools.partial(_b1_body, nh=nh),
        grid=(r // tm,),
        in_specs=[pl.BlockSpec((tm, d), row), full(kvn), full(bpre), full(wc), full(wka), full(wkb),
                  full(latn), full(wk), full(wv), full(wdq), full(qn), full(wq),
                  pl.BlockSpec((tm, HEAD_PAD), row), pl.BlockSpec((tm, HEAD_PAD), row),
                  pl.BlockSpec((tm, HEAD_PAD), row)],
        out_specs=[pl.BlockSpec((tm, nh * HEAD_PAD), row), pl.BlockSpec((tm, nh * HEAD_PAD), row),
                   pl.BlockSpec((tm, nh * V_HEAD), row)],
        out_shape=[jax.ShapeDtypeStruct((r, nh * HEAD_PAD), BF16),
                   jax.ShapeDtypeStruct((r, nh * HEAD_PAD), BF16),
                   jax.ShapeDtypeStruct((r, nh * V_HEAD), BF16)],
        compiler_params=_cparams(("arbitrary",)),
        name="b1_qkv_proj",
    )(h2, kvn, bpre, wc, wka, wkb, latn, wk, wv, wdq, qn, wq, tq, tkc, tks)


def _b2_body(q_ref, k_ref, v_ref, o_ref, *, n_meta, nq, blk):
    o_ref[0:n_meta, :] = jnp.zeros((n_meta, o_ref.shape[1]), o_ref.dtype)
    lane = lax.broadcasted_iota(I32, (blk, 2 * V_HEAD), 1)
    rowi = lax.broadcasted_iota(I32, (blk, blk), 0)
    coli = lax.broadcasted_iota(I32, (blk, blk), 1)
    causal = coli <= rowi
    neg = jnp.finfo(F32).min
    v_meta = v_ref[0:n_meta, :]

    def online(s, vblk, m, l, acc):
        m_new = jnp.maximum(m, jnp.max(s, axis=-1, keepdims=True))
        alpha = jnp.exp(m - m_new)
        p = jnp.exp(s - m_new)
        l = alpha * l + jnp.sum(p, axis=-1, keepdims=True)
        acc = alpha * acc + _dot(p.astype(BF16), vblk)
        return m_new, l, acc

    def qblock(i, _):
        q0 = pl.multiple_of(n_meta + i * blk, n_meta)
        qs = [q_ref[pl.ds(q0, blk), hh * HEAD_PAD:(hh + 1) * HEAD_PAD] for hh in range(2)]
        state = []
        for hh in range(2):
            s = _dot_t(qs[hh], k_ref[0:n_meta, hh * HEAD_PAD:(hh + 1) * HEAD_PAD])
            m = jnp.max(s, axis=-1, keepdims=True)
            p = jnp.exp(s - m)
            l = jnp.sum(p, axis=-1, keepdims=True)
            acc = _dot(p.astype(BF16), v_meta)
            state += [m, l, acc]

        def kv_step(j, st):
            k0 = pl.multiple_of(n_meta + j * blk, n_meta)
            vblk = v_ref[pl.ds(k0, blk), :]
            out = []
            for hh in range(2):
                s = _dot_t(qs[hh], k_ref[pl.ds(k0, blk), hh * HEAD_PAD:(hh + 1) * HEAD_PAD])
                out += list(online(s, vblk, *st[3 * hh:3 * hh + 3]))
            return tuple(out)

        state = lax.fori_loop(0, i, kv_step, tuple(state))
        vblk = v_ref[pl.ds(q0, blk), :]
        res = []
        for hh in range(2):
            s = _dot_t(qs[hh], k_ref[pl.ds(q0, blk), hh * HEAD_PAD:(hh + 1) * HEAD_PAD])
            s = jnp.where(causal, s, neg)
            m, l, acc = online(s, vblk, *state[3 * hh:3 * hh + 3])
            res.append(acc / l)
        o_ref[pl.ds(q0, blk), :] = jnp.where(lane < V_HEAD, res[0], res[1]).astype(o_ref.dtype)
        return 0

    lax.fori_loop(0, nq, qblock, 0)


def _b2(q2, k2, v2, *, bsz, nh, n_meta, nq):
    t = q2.shape[0]
    npair = nh // 2
    qk = pl.BlockSpec((t, 2 * HEAD_PAD), lambda b, p: (0, b * npair + p))
    vo = pl.BlockSpec((t, 2 * V_HEAD), lambda b, p: (0, b * npair + p))
    return pl.pallas_call(
        functools.partial(_b2_body, n_meta=n_meta, nq=nq, blk=ATT_BLK),
        grid=(bsz, npair),
        in_specs=[qk, qk, vo],
        out_specs=vo,
        out_shape=jax.ShapeDtypeStruct(v2.shape, BF16),
        compiler_params=_cparams(("arbitrary", "arbitrary")),
        name="b2_attention",
    )(q2, k2, v2)


def _b3_body(o_ref, h_ref, wo_ref, post_ref, mpre_ref, rw_ref, rb_ref, h3_ref, xn_ref, route_ref):
    mix = _dot(o_ref[...], wo_ref[...])
    h3 = h_ref[...] + _rms(mix, post_ref[...])
    h3_ref[...] = h3
    xn = _rms(h3, mpre_ref[...])
    xn_ref[...] = xn
    logits = _dot(xn.astype(BF16), rw_ref[...]) + rb_ref[...]
    lane = lax.broadcasted_iota(I32, logits.shape, 1).astype(F32)
    big = float(LANES)
    m1 = jnp.max(logits, axis=-1, keepdims=True)
    i1 = jnp.min(jnp.where(logits == m1, lane, big), axis=-1, keepdims=True)
    rest = jnp.where(lane == i1, -jnp.inf, logits)
    m2 = jnp.max(rest, axis=-1, keepdims=True)
    i2 = jnp.min(jnp.where(rest == m2, lane, big), axis=-1, keepdims=True)
    e2 = jnp.exp(m2 - m1)
    g1 = 1.0 / (1.0 + e2)
    g2 = e2 / (1.0 + e2)
    route_ref[...] = jnp.where(lane == 0.0, g1, jnp.where(lane == 1.0, g2,
                               jnp.where(lane == 2.0, i1, jnp.where(lane == 3.0, i2, 0.0))))


def _b3(o, h2, w_o, post, mpre, rw, rb, *, tm, blk0, nblk):
    d = h2.shape[1]
    src = lambda i: (i + blk0, 0)
    dst = lambda i: (i, 0)
    fixed = lambda i: (0, 0)
    rows = nblk * tm
    return pl.pallas_call(
        _b3_body,
        grid=(nblk,),
        in_specs=[pl.BlockSpec((tm, o.shape[1]), src), pl.BlockSpec((tm, d), src),
                  pl.BlockSpec(w_o.shape, fixed), pl.BlockSpec((1, d), fixed), pl.BlockSpec((1, d), fixed),
                  pl.BlockSpec(rw.shape, fixed), pl.BlockSpec((1, LANES), fixed)],
        out_specs=[pl.BlockSpec((tm, d), dst), pl.BlockSpec((tm, d), dst), pl.BlockSpec((tm, LANES), dst)],
        out_shape=[jax.ShapeDtypeStruct((rows, d), F32), jax.ShapeDtypeStruct((rows, d), F32),
                   jax.ShapeDtypeStruct((rows, LANES), F32)],
        compiler_params=_cparams(("arbitrary",)),
        name="b3_oproj_router",
    )(o, h2, w_o, post, mpre, rw, rb)


def _b4_body(route_ref, meta_ref, cnt_ref, run_s):
    i = pl.program_id(0)

    @pl.when(i == 0)
    def _():
        run_s[...] = jnp.zeros_like(run_s)

    route = route_ref[...]
    tm = route.shape[0]
    lane = lax.broadcasted_iota(I32, route.shape, 1).astype(F32)
    i1 = jnp.sum(jnp.where(lane == 2.0, route, 0.0), axis=-1, keepdims=True)
    i2 = jnp.sum(jnp.where(lane == 3.0, route, 0.0), axis=-1, keepdims=True)
    oh1 = lane == i1
    oh2 = lane == i2
    onehot = jnp.where(oh1 | oh2, 1.0, 0.0)
    rr = lax.broadcasted_iota(I32, (tm, tm), 0)
    cc = lax.broadcasted_iota(I32, (tm, tm), 1)
    lower = jnp.where(cc < rr, 1.0, 0.0).astype(BF16)
    run = run_s[...]
    rank = _dot(lower, onehot.astype(BF16)) + run
    r1 = jnp.sum(jnp.where(oh1, rank, 0.0), axis=-1, keepdims=True)
    r2 = jnp.sum(jnp.where(oh2, rank, 0.0), axis=-1, keepdims=True)
    packed = jnp.where(lane == 0.0, i1, jnp.where(lane == 1.0, i2,
                       jnp.where(lane == 2.0, r1, jnp.where(lane == 3.0, r2, 0.0))))
    meta_ref[0] = jnp.transpose(packed)[0:8, :].astype(I32)
    run = run + jnp.sum(onehot, axis=0, keepdims=True)
    run_s[...] = run
    cnt_ref[...] = jnp.broadcast_to(run, cnt_ref.shape).astype(I32)


def _b4(route, *, tm):
    nblk = route.shape[0] // tm
    return pl.pallas_call(
        _b4_body,
        grid=(nblk,),
        in_specs=[pl.BlockSpec((tm, LANES), lambda i: (i, 0))],
        out_specs=[pl.BlockSpec((1, 8, tm), lambda i: (i, 0, 0)), pl.BlockSpec((8, LANES), lambda i: (0, 0))],
        out_shape=[jax.ShapeDtypeStruct((nblk, 8, tm), I32), jax.ShapeDtypeStruct((8, LANES), I32)],
        scratch_shapes=[pltpu.VMEM((1, LANES), F32)],
        compiler_params=_cparams(("arbitrary",)),
        name="b4_route_rank",
    )(route)


def _b5_body(off_ref, meta_ref, xn_ref, xg_in_ref, xg_ref, sem, *, tm):
    del xg_in_ref
    base = pl.program_id(0) * tm

    def copy(r, k):
        pos = off_ref[meta_ref[0, k, r]] + meta_ref[0, 2 + k, r]
        return pltpu.make_async_copy(xn_ref.at[pl.ds(base + r, 1)], xg_ref.at[pl.ds(pos, 1)], sem)

    def issue(r, _):
        copy(r, 0).start()
        copy(r, 1).start()
        return 0

    def drain(r, _):
        copy(r, 0).wait()
        copy(r, 1).wait()
        return 0

    lax.fori_loop(0, tm, issue, 0)
    lax.fori_loop(0, tm, drain, 0)


def _b5(off, meta, xn, xg0, *, tm):
    nblk = meta.shape[0]
    return pl.pallas_call(
        functools.partial(_b5_body, tm=tm),
        grid_spec=pltpu.PrefetchScalarGridSpec(
            num_scalar_prefetch=1,
            grid=(nblk,),
            in_specs=[pl.BlockSpec((1, 8, tm), lambda i, off: (i, 0, 0), memory_space=pltpu.SMEM),
                      pl.BlockSpec(memory_space=pl.ANY), pl.BlockSpec(memory_space=pl.ANY)],
            out_specs=pl.BlockSpec(memory_space=pl.ANY),
            scratch_shapes=[pltpu.SemaphoreType.DMA(())]),
        out_shape=jax.ShapeDtypeStruct(xg0.shape, xg0.dtype),
        input_output_aliases={3: 0},
        compiler_params=_cparams(("arbitrary",)),
        name="b5_dispatch",
    )(off, meta, xn, xg0)


def _b6_body(te_ref, tv_ref, x_ref, wg_ref, wu_ref, wd_ref, y_ref, xb_s, acc_s):
    j = pl.program_id(0)
    f = pl.program_id(1)

    @pl.when(f == 0)
    def _():
        xb_s[...] = x_ref[...].astype(BF16)
        acc_s[...] = jnp.zeros_like(acc_s)

    @pl.when(tv_ref[j] > 0)
    def _():
        xb = xb_s[...]
        g = _dot(xb, wg_ref[0])
        u = _dot(xb, wu_ref[0])
        acc_s[...] += _dot((jax.nn.silu(g) * u).astype(BF16), wd_ref[0])

    @pl.when(f == pl.num_programs(1) - 1)
    def _():
        y_ref[...] = acc_s[...]


def _b6(te, tv, xg, w_gu, w_d, *, tm, tf):
    rg, d = xg.shape
    eff = w_d.shape[1]
    nf = eff // tf
    nt = rg // tm
    return pl.pallas_call(
        _b6_body,
        grid_spec=pltpu.PrefetchScalarGridSpec(
            num_scalar_prefetch=2,
            grid=(nt, nf),
            in_specs=[pl.BlockSpec((tm, d), lambda j, f, te, tv: (j, 0)),
                      pl.BlockSpec((1, d, tf), lambda j, f, te, tv: (te[j], 0, f)),
                      pl.BlockSpec((1, d, tf), lambda j, f, te, tv: (te[j], 0, nf + f)),
                      pl.BlockSpec((1, tf, d), lambda j, f, te, tv: (te[j], f, 0))],
            out_specs=pl.BlockSpec((tm, d), lambda j, f, te, tv: (j, 0)),
            scratch_shapes=[pltpu.VMEM((tm, d), BF16), pltpu.VMEM((tm, d), F32)]),
        out_shape=jax.ShapeDtypeStruct((rg, d), F32),
        compiler_params=_cparams(("arbitrary", "arbitrary")),
        name="b6_moe_ffn",
    )(te, tv, xg, w_gu, w_gu, w_d)


def _b7_body(off_ref, meta_ref, route_ref, h_ref, post_ref, yg_ref, out_ref, buf_s, sem, *, tm):
    def copy(r, k):
        pos = off_ref[meta_ref[0, k, r]] + meta_ref[0, 2 + k, r]
        return pltpu.make_async_copy(yg_ref.at[pl.ds(pos, 1)], buf_s.at[k, pl.ds(r, 1)], sem)

    def issue(r, _):
        copy(r, 0).start()
        copy(r, 1).start()
        return 0

    def drain(r, _):
        copy(r, 0).wait()
        copy(r, 1).wait()
        return 0

    lax.fori_loop(0, tm, issue, 0)
    lax.fori_loop(0, tm, drain, 0)
    route = route_ref[...]
    y = route[:, 0:1] * buf_s[0] + route[:, 1:2] * buf_s[1]
    out_ref[...] = h_ref[...] + _rms(y, post_ref[...])


def _b7(off, meta, route, h3, post, yg, *, tm):
    nblk = meta.shape[0]
    d = h3.shape[1]
    row = lambda i, off: (i, 0)
    return pl.pallas_call(
        functools.partial(_b7_body, tm=tm),
        grid_spec=pltpu.PrefetchScalarGridSpec(
            num_scalar_prefetch=1,
            grid=(nblk,),
            in_specs=[pl.BlockSpec((1, 8, tm), lambda i, off: (i, 0, 0), memory_space=pltpu.SMEM),
                      pl.BlockSpec((tm, LANES), row), pl.BlockSpec((tm, d), row),
                      pl.BlockSpec((1, d), lambda i, off: (0, 0)),
                      pl.BlockSpec(memory_space=pl.ANY)],
            out_specs=pl.BlockSpec((tm, d), row),
            scratch_shapes=[pltpu.VMEM((2, tm, d), F32), pltpu.SemaphoreType.DMA(())]),
        out_shape=jax.ShapeDtypeStruct((nblk * tm, d), F32),
        compiler_params=_cparams(("arbitrary",)),
        name="b7_combine",
    )(off, meta, route, h3, post, yg)


def _pick(n, target, mult):
    best = mult
    for cand in range(mult, min(n, target) + 1, mult):
        if n % cand == 0:
            best = cand
    return best


def kernel(x, meta_tokens, a_pre_norm, a_w_in, a_conv_w, a_conv_b, a_gate_a_w, a_gate_a_b, a_gate_x_w, a_gate_x_b, a_lambda, a_w_out, a_post_norm, kv_norm, kv_w_dkv, kv_latent_norm, kv_w_ukv, b_pre_norm, b_w_dq, b_q_norm, b_w_uq, b_w_o, b_post_norm, f_pre_norm, f_w_gate_up, f_w_down, f_post_norm, m_pre_norm, m_router_w, m_router_b, m_w_gate_up, m_w_down, m_post_norm):
    bsz, seq, d = x.shape
    n_meta = meta_tokens.shape[0]
    assert a_pre_norm.shape[0] == 1 and b_pre_norm.shape[0] == 1, "one RG-LRU layer + one MLA layer"
    t_all = n_meta + seq
    assert t_all % TT == 0 and seq % ATT_BLK == 0 and bsz % 8 == 0 and n_meta % 16 == 0
    rows = t_all * bsz
    row = lambda v: v.reshape(1, -1).astype(F32)
    bf = lambda w: w.astype(BF16)

    h0 = jnp.concatenate([jnp.broadcast_to(meta_tokens.astype(F32)[:, None, :], (n_meta, bsz, d)),
                          jnp.transpose(x, (1, 0, 2))], axis=0).reshape(rows, d)

    y, xr = _a1(h0, row(a_pre_norm[0]), bf(a_w_in[0]), tm=TT * bsz)
    wg = bf(jnp.concatenate([a_gate_a_w[0], a_gate_x_w[0]], axis=-1))
    h1 = _a2(xr, y, h0, a_conv_w[0].astype(F32), row(a_conv_b[0]), wg, row(a_gate_a_b[0]), row(a_gate_x_b[0]),
             row(a_lambda[0]), bf(a_w_out[0]), row(a_post_norm[0]), tt=TT_SMALL, bsz=bsz)
    d_ff = f_w_down.shape[1]
    h2 = _a3(h1, row(f_pre_norm[0]), bf(f_w_gate_up[0]), bf(f_w_down[0]), row(f_post_norm[0]),
             tm=TT * bsz, tf=_pick(d_ff, 1024, LANES))

    kv_lora = kv_latent_norm.shape[0]
    nh = b_w_uq.shape[2] // (QK_NOPE + QK_ROPE)
    hr = QK_ROPE // 2
    scale = (QK_NOPE + QK_ROPE) ** -0.5
    inv = ROPE_THETA ** (-jnp.arange(0, QK_ROPE, 2, dtype=F32) / QK_ROPE)
    ang = jnp.arange(t_all, dtype=F32)[:, None] * inv[None, :]
    cos, sin = jnp.cos(ang), jnp.sin(ang)
    one = jnp.ones((t_all, QK_NOPE), F32)
    zero = jnp.zeros((t_all, QK_NOPE), F32)
    flat = lambda tab: jnp.repeat(tab, bsz, axis=0)
    tq = flat(scale * jnp.concatenate([one, cos, cos, -sin, sin], axis=1))
    tkc = flat(jnp.concatenate([zero, cos, cos, cos, cos], axis=1))
    tks = flat(jnp.concatenate([zero, -sin, sin, -sin, sin], axis=1))

    w_c = kv_w_dkv[:, :kv_lora]
    k1 = kv_w_dkv[:, kv_lora:kv_lora + hr]
    k2 = kv_w_dkv[:, kv_lora + hr:]
    zk = jnp.zeros((d, QK_NOPE), F32)
    wka = jnp.concatenate([zk, k1, k2, k1, k2], axis=1)
    wkb = jnp.concatenate([zk, k2, k1, k2, k1], axis=1)
    ukv = kv_w_ukv.reshape(kv_lora, nh, QK_NOPE + V_HEAD)
    wk = jnp.concatenate([ukv[:, :, :QK_NOPE], jnp.zeros((kv_lora, nh, HEAD_PAD - QK_NOPE), F32)],
                         axis=2).reshape(kv_lora, nh * HEAD_PAD)
    wv = ukv[:, :, QK_NOPE:].reshape(kv_lora, nh * V_HEAD)
    uq = b_w_uq[0].reshape(-1, nh, QK_NOPE + QK_ROPE)
    qn_, q1, q2 = uq[:, :, :QK_NOPE], uq[:, :, QK_NOPE:QK_NOPE + hr], uq[:, :, QK_NOPE + hr:]
    wq = jnp.concatenate([qn_, q1, q2, q2, q1], axis=2).reshape(-1, nh * HEAD_PAD)
    q, k, v = _b1(h2, row(kv_norm), row(b_pre_norm[0]), bf(w_c), bf(wka), bf(wkb), row(kv_latent_norm),
                  bf(wk), bf(wv), bf(b_w_dq[0]), row(b_q_norm[0]), bf(wq), tq, tkc, tks,
                  tm=TT_SMALL * bsz, nh=nh)

    o2 = _b2(q.reshape(t_all, bsz * nh * HEAD_PAD), k.reshape(t_all, bsz * nh * HEAD_PAD),
             v.reshape(t_all, bsz * nh * V_HEAD), bsz=bsz, nh=nh, n_meta=n_meta, nq=seq // ATT_BLK)
    o = o2.reshape(rows, nh * V_HEAD)

    rows_seq = seq * bsz
    tm1 = n_meta * bsz
    assert rows_seq % tm1 == 0 and rows_seq % MOE_TILE == 0
    ne = m_router_w.shape[2]
    rw = bf(jnp.concatenate([m_router_w[0], jnp.zeros((d, LANES - ne), F32)], axis=1))
    rb = jnp.concatenate([m_router_b[0].astype(F32), jnp.full((LANES - ne,), -1e30, F32)]).reshape(1, LANES)
    h3, xn, route = _b3(o, h2, bf(b_w_o[0]), row(b_post_norm[0]), row(m_pre_norm[0]), rw, rb,
                        tm=tm1, blk0=1, nblk=rows_seq // tm1)

    meta, cnt = _b4(route, tm=tm1)
    tmg = MOE_TILE
    counts = cnt[0, :ne]
    padded = ((counts + tmg - 1) // tmg) * tmg
    ends = jnp.cumsum(padded)
    off = (ends - padded).astype(I32)
    nt = (TOP_K * rows_seq) // tmg + ne
    tile_start = jnp.arange(nt, dtype=I32) * tmg
    tv = (tile_start < ends[-1]).astype(I32)
    last = jnp.maximum(ends[-1] // tmg - 1, 0)
    te_raw = jnp.sum((tile_start[:, None] >= ends[None, :]).astype(I32), axis=1)
    te = jnp.minimum(jnp.where(tv > 0, te_raw, te_raw[last]), ne - 1).astype(I32)

    xg = _b5(off, meta, xn, jnp.zeros((nt * tmg, d), F32), tm=tm1)
    e_ff = m_w_down.shape[2]
    yg = _b6(te, tv, xg, bf(m_w_gate_up[0]), bf(m_w_down[0]), tm=tmg, tf=_pick(e_ff, 512, LANES))
    h4 = _b7(off, meta, route, h3, row(m_post_norm[0]), yg, tm=tm1)
    return jnp.transpose(h4.reshape(seq, bsz, d), (1, 0, 2))
```

```python
import functools
import math

import jax
import jax.numpy as jnp
from jax import lax
from jax.experimental import pallas as pl
from jax.experimental.pallas import tpu as pltpu

F32 = jnp.float32
BF16 = jnp.bfloat16
I32 = jnp.int32

RMS_EPS = 1e-6
RG_C = 8.0
ROPE_THETA = 10000.0
QK_NOPE = 64
QK_ROPE = 32
V_HEAD = 64
HEAD_PAD = 128
TOP_K = 2
LANES = 128

TT = 48
TT_SMALL = 24
PAD_T = 240
ATT_BLK = 256
ATT_QBLK = 1024
MOE_TILE = 512
VMEM_LIMIT = 56 * 1024 * 1024


def _cparams(sem):
    return pltpu.CompilerParams(dimension_semantics=sem, vmem_limit_bytes=VMEM_LIMIT)


def _rms(x, g):
    ms = jnp.mean(x * x, axis=-1, keepdims=True)
    return x * lax.rsqrt(ms + RMS_EPS) * g


def _dot(a, b):
    return jnp.dot(a, b, preferred_element_type=F32)


def _dot_t(a, b):
    return lax.dot_general(a, b, (((1,), (1,)), ((), ())), preferred_element_type=F32)


def _a1_body(h_ref, g_ref, w_ref, y_ref, xr_ref, *, c):
    xn = _rms(h_ref[...], g_ref[...]).astype(BF16)
    y_ref[...] = _dot(xn, w_ref[:, :c]).astype(BF16)
    xr_ref[...] = _dot(xn, w_ref[:, c:]).astype(BF16)


def _a1(h0, gain, w_in, *, tm):
    r, d = h0.shape
    c = w_in.shape[1] // 2
    row = lambda i: (i, 0)
    fixed = lambda i: (0, 0)
    return pl.pallas_call(
        functools.partial(_a1_body, c=c),
        grid=(r // tm,),
        in_specs=[pl.BlockSpec((tm, d), row), pl.BlockSpec((1, d), fixed),
                  pl.BlockSpec((d, 2 * c), fixed)],
        out_specs=[pl.BlockSpec((tm, c), row), pl.BlockSpec((tm, c), row)],
        out_shape=[jax.ShapeDtypeStruct((r, c), BF16), jax.ShapeDtypeStruct((r, c), BF16)],
        compiler_params=_cparams(("arbitrary",)),
        name="a1_norm_win",
    )(h0, gain, w_in)


def _a2_body(xr_ref, y_ref, h_ref, cw_ref, cb_ref, wg_ref, gab_ref, gxb_ref, lam_ref,
             wo_ref, pn_ref, out_ref, ext_s, a_s, b_s, carry_s, *, nb, tt, bsz, cw):
    tm = tt * bsz
    tail = (cw - 1) * bsz

    @pl.when(pl.program_id(0) == 0)
    def _():
        ext_s[0:tail, :] = jnp.zeros((tail, ext_s.shape[1]), F32)
        carry_s[...] = jnp.zeros_like(carry_s)

    ext_s[tail:tail + tm, :] = xr_ref[...].astype(F32)
    xc = cb_ref[...] + ext_s[0:tm, :] * cw_ref[0:1, :]
    for k in range(1, cw):
        xc = xc + ext_s[k * bsz:k * bsz + tm, :] * cw_ref[k:k + 1, :]
    ext_s[0:tail, :] = ext_s[tm:tm + tail, :]

    neg_lam = -lam_ref[...]
    sp = jnp.maximum(neg_lam, 0.0) + jnp.log1p(jnp.exp(-jnp.abs(neg_lam)))
    xcb = xc.astype(BF16)
    for n in range(nb):
        sl = slice(n * LANES, (n + 1) * LANES)
        g = _dot(xcb[:, sl], wg_ref[n])
        r = jax.nn.sigmoid(g[:, :LANES] + gab_ref[:, sl])
        ig = jax.nn.sigmoid(g[:, LANES:] + gxb_ref[:, sl])
        log_a = (-RG_C) * r * sp[:, sl]
        a = jnp.exp(log_a)
        a_s[:, sl] = a
        b_s[:, sl] = jnp.sqrt(1.0 - a * a) * (ig * xc[:, sl])

    def step(t, h):
        rows = pl.ds(pl.multiple_of(t * bsz, bsz), bsz)
        h = a_s[rows, :] * h + b_s[rows, :]
        b_s[rows, :] = h
        return h

    carry_s[...] = lax.fori_loop(0, tt, step, carry_s[...], unroll=4)

    z = (jax.nn.gelu(y_ref[...].astype(F32)) * b_s[...]).astype(BF16)
    mix = _dot(z, wo_ref[...])
    out_ref[...] = h_ref[...] + _rms(mix, pn_ref[...])


def _a2(xr, y, h0, conv_w, conv_b, wg, ga_b, gx_b, lam, w_out, post, *, tt, bsz):
    r, c = xr.shape
    d = h0.shape[1]
    nb = wg.shape[0]
    cw = conv_w.shape[0]
    tm = tt * bsz
    row = lambda i: (i, 0)
    fixed2 = lambda i: (0, 0)
    fixed3 = lambda i: (0, 0, 0)
    return pl.pallas_call(
        functools.partial(_a2_body, nb=nb, tt=tt, bsz=bsz, cw=cw),
        grid=(r // tm,),
        in_specs=[pl.BlockSpec((tm, c), row), pl.BlockSpec((tm, c), row), pl.BlockSpec((tm, d), row),
                  pl.BlockSpec((cw, c), fixed2), pl.BlockSpec((1, c), fixed2),
                  pl.BlockSpec((nb, LANES, 2 * LANES), fixed3),
                  pl.BlockSpec((1, c), fixed2), pl.BlockSpec((1, c), fixed2), pl.BlockSpec((1, c), fixed2),
                  pl.BlockSpec((c, d), fixed2), pl.BlockSpec((1, d), fixed2)],
        out_specs=pl.BlockSpec((tm, d), row),
        out_shape=jax.ShapeDtypeStruct((r, d), F32),
        scratch_shapes=[pltpu.VMEM((tm + (cw - 1) * bsz, c), F32), pltpu.VMEM((tm, c), F32),
                        pltpu.VMEM((tm, c), F32), pltpu.VMEM((bsz, c), F32)],
        compiler_params=_cparams(("arbitrary",)),
        name="a2_rglru",
    )(xr, y, h0, conv_w, conv_b, wg, ga_b, gx_b, lam, w_out, post)


def _a3_body(h_ref, pre_ref, wg_ref, wu_ref, wd_ref, post_ref, out_ref, xn_s, acc_s, tr_s, *, npad, bsz, tt):
    i = pl.program_id(0)
    f = pl.program_id(1)
    last = f == pl.num_programs(1) - 1

    @pl.when((i < npad) & last)
    def _():
        out_ref[...] = jnp.zeros_like(out_ref)

    @pl.when(i >= npad)
    def _():
        @pl.when(f == 0)
        def _():
            xn_s[...] = _rms(h_ref[...], pre_ref[...]).astype(BF16)
            acc_s[...] = jnp.zeros_like(acc_s)

        xn = xn_s[...]
        g = _dot(xn, wg_ref[...])
        u = _dot(xn, wu_ref[...])
        acc_s[...] += _dot((jax.nn.silu(g) * u).astype(BF16), wd_ref[...])

        @pl.when(last)
        def _():
            res = h_ref[...] + _rms(acc_s[...], post_ref[...])
            for c in range(tr_s.shape[0]):
                tr_s[c] = res[:, c * LANES:(c + 1) * LANES]
            for b in range(bsz):
                for c in range(tr_s.shape[0]):
                    out_ref[b, :, c * LANES:(c + 1) * LANES] = tr_s[c, pl.ds(b, tt, stride=bsz), :]


def _a3(h1, pre, w_gu, w_d, post, *, bsz, tt, npad, tf):
    r, d = h1.shape
    tm = tt * bsz
    ff = w_d.shape[0]
    nf = ff // tf
    nblk = r // tm
    row = lambda i, f: (jnp.maximum(i - npad, 0), 0)
    fixed = lambda i, f: (0, 0)
    fsel = lambda i, f: jnp.where(i < npad, 0, f)
    return pl.pallas_call(
        functools.partial(_a3_body, npad=npad, bsz=bsz, tt=tt),
        grid=(npad + nblk, nf),
        in_specs=[pl.BlockSpec((tm, d), row), pl.BlockSpec((1, d), fixed),
                  pl.BlockSpec((d, tf), lambda i, f: (0, fsel(i, f))),
                  pl.BlockSpec((d, tf), lambda i, f: (0, nf + fsel(i, f))),
                  pl.BlockSpec((tf, d), lambda i, f: (fsel(i, f), 0)),
                  pl.BlockSpec((1, d), fixed)],
        out_specs=pl.BlockSpec((bsz, tt, d), lambda i, f: (0, i, 0)),
        out_shape=jax.ShapeDtypeStruct((bsz, (npad + nblk) * tt, d), F32),
        scratch_shapes=[pltpu.VMEM((tm, d), BF16), pltpu.VMEM((tm, d), F32),
                        pltpu.VMEM((d // LANES, tm, LANES), F32)],
        compiler_params=_cparams(("arbitrary", "arbitrary")),
        name="a3_dense_ffn",
    )(h1, pre, w_gu, w_gu, w_d, post)


def _b1_body(h_ref, kvn_ref, bpre_ref, wc_ref, wka_ref, wkb_ref, latn_ref, wk_ref, wv_ref,
             wdq_ref, qn_ref, wq_ref, tq_ref, tkc_ref, tks_ref, q_ref, k_ref, v_ref, *, nh):
    h = h_ref[...]
    hn = h * lax.rsqrt(jnp.mean(h * h, axis=-1, keepdims=True) + RMS_EPS)
    ukv = (hn * kvn_ref[...]).astype(BF16)
    uq = (hn * bpre_ref[...]).astype(BF16)

    ckv = _rms(_dot(ukv, wc_ref[...]), latn_ref[...]).astype(BF16)
    kr = _dot(ukv, wka_ref[...]) * tkc_ref[...] + _dot(ukv, wkb_ref[...]) * tks_ref[...]
    kn = _dot(ckv, wk_ref[...])
    for hd in range(nh):
        sl = slice(hd * HEAD_PAD, (hd + 1) * HEAD_PAD)
        k_ref[:, sl] = (kn[:, sl] + kr).astype(BF16)
    v_ref[...] = _dot(ckv, wv_ref[...]).astype(BF16)

    cq = _rms(_dot(uq, wdq_ref[...]), qn_ref[...]).astype(BF16)
    qr = _dot(cq, wq_ref[...])
    tq = tq_ref[...]
    for hd in range(nh):
        sl = slice(hd * HEAD_PAD, (hd + 1) * HEAD_PAD)
        q_ref[:, sl] = (qr[:, sl] * tq).astype(BF16)


def _b1(h2, kvn, bpre, wc, wka, wkb, latn, wk, wv, wdq, qn, wq, tq, tkc, tks, *, tm, nh):
    r, d = h2.shape
    nper = tq.shape[0] // tm
    row = lambda i: (i, 0)
    tab = lambda i: (i % nper, 0)
    fixed = lambda i: (0, 0)
    full = lambda a: pl.BlockSpec(a.shape, fixed)
    return pl.pallas_call(
        functools.partial(_b1_body, nh=nh),
        grid=(r // tm,),
        in_specs=[pl.BlockSpec((tm, d), row), full(kvn), full(bpre), full(wc), full(wka), full(wkb),
                  full(latn), full(wk), full(wv), full(wdq), full(qn), full(wq),
                  pl.BlockSpec((tm, HEAD_PAD), tab), pl.BlockSpec((tm, HEAD_PAD), tab),
                  pl.BlockSpec((tm, HEAD_PAD), tab)],
        out_specs=[pl.BlockSpec((tm, nh * HEAD_PAD), row), pl.BlockSpec((tm, nh * HEAD_PAD), row),
                   pl.BlockSpec((tm, nh * V_HEAD), row)],
        out_shape=[jax.ShapeDtypeStruct((r, nh * HEAD_PAD), BF16),
                   jax.ShapeDtypeStruct((r, nh * HEAD_PAD), BF16),
                   jax.ShapeDtypeStruct((r, nh * V_HEAD), BF16)],
        compiler_params=_cparams(("arbitrary",)),
        name="b1_qkv_proj",
    )(h2, kvn, bpre, wc, wka, wkb, latn, wk, wv, wdq, qn, wq, tq, tkc, tks)


def _tile2(x):
    return jnp.concatenate([x, x], axis=1)


def _b2_body(q_ref, k_ref, v_ref, o_ref, vext_s, m0_s, m1_s, acc0_s, acc1_s, *, meta0, seq0, nq, qb, kb):
    pair = 2 * V_HEAD
    t_pad = v_ref.shape[1]
    m_s = (m0_s, m1_s)
    acc_s = (acc0_s, acc1_s)
    vext_s[meta0:, 0:pair] = v_ref[0, meta0:, :]
    vext_s[meta0:, pair:] = jnp.ones((t_pad - meta0, pair), BF16)
    o_ref[0, 0:seq0, :] = jnp.zeros((seq0, pair), o_ref.dtype)
    lane = lax.broadcasted_iota(I32, (qb, pair), 1)
    neg = jnp.finfo(F32).min

    def update(hh, qh, qlo, krows, masked):
        s = _dot_t(qh[qlo:], k_ref[0, krows, hh * HEAD_PAD:(hh + 1) * HEAD_PAD])
        if masked:
            rowi = lax.broadcasted_iota(I32, s.shape, 0)
            coli = lax.broadcasted_iota(I32, s.shape, 1)
            s = jnp.where(coli <= rowi, s, neg)
        m_old = m_s[hh][qlo:, :]
        m_new = jnp.maximum(m_old, jnp.max(s, axis=-1, keepdims=True))
        alpha = jnp.exp2(m_old - m_new)
        p = jnp.exp2(s - _tile2(m_new))
        acc_s[hh][qlo:, :] = _tile2(alpha) * acc_s[hh][qlo:, :] + _dot(p.astype(BF16), vext_s[krows, :])
        m_s[hh][qlo:, :] = m_new

    def qblock(i, _):
        q0 = pl.multiple_of(seq0 + i * qb, kb)
        qs = [q_ref[0, pl.ds(q0, qb), hh * HEAD_PAD:(hh + 1) * HEAD_PAD] for hh in range(2)]
        for hh in range(2):
            s = _dot_t(qs[hh], k_ref[0, meta0:seq0, hh * HEAD_PAD:(hh + 1) * HEAD_PAD])
            m = jnp.max(s, axis=-1, keepdims=True)
            p = jnp.exp2(s - m)
            acc_s[hh][...] = _dot(p.astype(BF16), vext_s[meta0:seq0, :])
            m_s[hh][...] = jnp.broadcast_to(m, (qb, pair))

        def kv_step(j, _):
            krows = pl.ds(pl.multiple_of(seq0 + j * kb, kb), kb)
            for hh in range(2):
                update(hh, qs[hh], 0, krows, False)
            return 0

        lax.fori_loop(0, i * (qb // kb), kv_step, 0)
        for jj in range(qb // kb):
            krows = pl.ds(pl.multiple_of(q0 + jj * kb, kb), kb)
            for hh in range(2):
                update(hh, qs[hh], jj * kb, krows, True)
        res = []
        for hh in range(2):
            acc = acc_s[hh][...]
            res.append(acc[:, :pair] / acc[:, pair:])
        o_ref[0, pl.ds(q0, qb), :] = jnp.where(lane < V_HEAD, res[0], res[1]).astype(o_ref.dtype)
        return 0

    lax.fori_loop(0, nq, qblock, 0)


def _b2(q, k, v, *, nh, meta0, seq0, seq, qb):
    bsz, t_pad, _ = q.shape
    npair = nh // 2
    qk = pl.BlockSpec((1, t_pad, 2 * HEAD_PAD), lambda b, p: (b, 0, p))
    vo = pl.BlockSpec((1, t_pad, 2 * V_HEAD), lambda b, p: (b, 0, p))
    return pl.pallas_call(
        functools.partial(_b2_body, meta0=meta0, seq0=seq0, nq=seq // qb, qb=qb, kb=ATT_BLK),
        grid=(bsz, npair),
        in_specs=[qk, qk, vo],
        out_specs=vo,
        out_shape=jax.ShapeDtypeStruct(v.shape, BF16),
        scratch_shapes=[pltpu.VMEM((t_pad, 4 * V_HEAD), BF16),
                        pltpu.VMEM((qb, 2 * V_HEAD), F32), pltpu.VMEM((qb, 2 * V_HEAD), F32),
                        pltpu.VMEM((qb, 4 * V_HEAD), F32), pltpu.VMEM((qb, 4 * V_HEAD), F32)],
        compiler_params=_cparams(("arbitrary", "arbitrary")),
        name="b2_attention",
    )(q, k, v)


def _b3_body(o_ref, h_ref, wo_ref, post_ref, mpre_ref, rw_ref, rb_ref, h3_ref, xn_ref, route_ref):
    mix = _dot(o_ref[0], wo_ref[...])
    h3 = h_ref[0] + _rms(mix, post_ref[...])
    h3_ref[...] = h3
    xn = _rms(h3, mpre_ref[...])
    xn_ref[...] = xn
    logits = _dot(xn.astype(BF16), rw_ref[...]) + rb_ref[...]
    lane = lax.broadcasted_iota(I32, logits.shape, 1).astype(F32)
    big = float(LANES)
    m1 = jnp.max(logits, axis=-1, keepdims=True)
    i1 = jnp.min(jnp.where(logits == m1, lane, big), axis=-1, keepdims=True)
    rest = jnp.where(lane == i1, -jnp.inf, logits)
    m2 = jnp.max(rest, axis=-1, keepdims=True)
    i2 = jnp.min(jnp.where(rest == m2, lane, big), axis=-1, keepdims=True)
    e2 = jnp.exp(m2 - m1)
    g1 = 1.0 / (1.0 + e2)
    g2 = e2 / (1.0 + e2)
    route_ref[...] = jnp.where(lane == 0.0, g1, jnp.where(lane == 1.0, g2,
                               jnp.where(lane == 2.0, i1, jnp.where(lane == 3.0, i2, 0.0))))


def _b3(o, h2, w_o, post, mpre, rw, rb, *, tm, blk0, nper):
    bsz, _, d = h2.shape
    src = lambda i: (i // nper, blk0 + i % nper, 0)
    dst = lambda i: (i, 0)
    fixed = lambda i: (0, 0)
    nblk = bsz * nper
    rows = nblk * tm
    return pl.pallas_call(
        _b3_body,
        grid=(nblk,),
        in_specs=[pl.BlockSpec((1, tm, o.shape[2]), src), pl.BlockSpec((1, tm, d), src),
                  pl.BlockSpec(w_o.shape, fixed), pl.BlockSpec((1, d), fixed), pl.BlockSpec((1, d), fixed),
                  pl.BlockSpec(rw.shape, fixed), pl.BlockSpec((1, LANES), fixed)],
        out_specs=[pl.BlockSpec((tm, d), dst), pl.BlockSpec((tm, d), dst), pl.BlockSpec((tm, LANES), dst)],
        out_shape=[jax.ShapeDtypeStruct((rows, d), F32), jax.ShapeDtypeStruct((rows, d), F32),
                   jax.ShapeDtypeStruct((rows, LANES), F32)],
        compiler_params=_cparams(("arbitrary",)),
        name="b3_oproj_router",
    )(o, h2, w_o, post, mpre, rw, rb)


def _b4_body(route_ref, meta_ref, cnt_ref, run_s):
    i = pl.program_id(0)

    @pl.when(i == 0)
    def _():
        run_s[...] = jnp.zeros_like(run_s)

    route = route_ref[...]
    tm = route.shape[0]
    lane = lax.broadcasted_iota(I32, route.shape, 1).astype(F32)
    i1 = jnp.sum(jnp.where(lane == 2.0, route, 0.0), axis=-1, keepdims=True)
    i2 = jnp.sum(jnp.where(lane == 3.0, route, 0.0), axis=-1, keepdims=True)
    oh1 = lane == i1
    oh2 = lane == i2
    onehot = jnp.where(oh1 | oh2, 1.0, 0.0)
    rr = lax.broadcasted_iota(I32, (tm, tm), 0)
    cc = lax.broadcasted_iota(I32, (tm, tm), 1)
    lower = jnp.where(cc < rr, 1.0, 0.0).astype(BF16)
    run = run_s[...]
    rank = _dot(lower, onehot.astype(BF16)) + run
    r1 = jnp.sum(jnp.where(oh1, rank, 0.0), axis=-1, keepdims=True)
    r2 = jnp.sum(jnp.where(oh2, rank, 0.0), axis=-1, keepdims=True)
    packed = jnp.where(lane == 0.0, i1, jnp.where(lane == 1.0, i2,
                       jnp.where(lane == 2.0, r1, jnp.where(lane == 3.0, r2, 0.0))))
    meta_ref[0] = jnp.transpose(packed)[0:8, :].astype(I32)
    run = run + jnp.sum(onehot, axis=0, keepdims=True)
    run_s[...] = run
    cnt_ref[...] = jnp.broadcast_to(run, cnt_ref.shape).astype(I32)


def _b4(route, *, tm):
    nblk = route.shape[0] // tm
    return pl.pallas_call(
        _b4_body,
        grid=(nblk,),
        in_specs=[pl.BlockSpec((tm, LANES), lambda i: (i, 0))],
        out_specs=[pl.BlockSpec((1, 8, tm), lambda i: (i, 0, 0)), pl.BlockSpec((8, LANES), lambda i: (0, 0))],
        out_shape=[jax.ShapeDtypeStruct((nblk, 8, tm), I32), jax.ShapeDtypeStruct((8, LANES), I32)],
        scratch_shapes=[pltpu.VMEM((1, LANES), F32)],
        compiler_params=_cparams(("arbitrary",)),
        name="b4_route_rank",
    )(route)


def _b5_body(off_ref, meta_ref, xn_ref, xg_in_ref, xg_ref, sem, *, tm):
    del xg_in_ref

    def copy(r, k):
        pos = off_ref[meta_ref[0, k, r]] + meta_ref[0, 2 + k, r]
        return pltpu.make_async_copy(xn_ref.at[pl.ds(r, 1)], xg_ref.at[pl.ds(pos, 1)], sem)

    def issue(r, _):
        copy(r, 0).start()
        copy(r, 1).start()
        return 0

    def drain(r, _):
        copy(r, 0).wait()
        copy(r, 1).wait()
        return 0

    lax.fori_loop(0, tm, issue, 0)
    lax.fori_loop(0, tm, drain, 0)


def _b5(off, meta, xn, xg0, *, tm):
    nblk = meta.shape[0]
    return pl.pallas_call(
        functools.partial(_b5_body, tm=tm),
        grid_spec=pltpu.PrefetchScalarGridSpec(
            num_scalar_prefetch=1,
            grid=(nblk,),
            in_specs=[pl.BlockSpec((1, 8, tm), lambda i, off: (i, 0, 0), memory_space=pltpu.SMEM),
                      pl.BlockSpec((tm, xn.shape[1]), lambda i, off: (i, 0)),
                      pl.BlockSpec(memory_space=pl.ANY)],
            out_specs=pl.BlockSpec(memory_space=pl.ANY),
            scratch_shapes=[pltpu.SemaphoreType.DMA(())]),
        out_shape=jax.ShapeDtypeStruct(xg0.shape, xg0.dtype),
        input_output_aliases={3: 0},
        compiler_params=_cparams(("arbitrary",)),
        name="b5_dispatch",
    )(off, meta, xn, xg0)


def _b6_body(te_ref, tv_ref, x_ref, wg_ref, wu_ref, wd_ref, y_ref, xb_s, acc_s):
    j = pl.program_id(0)
    f = pl.program_id(1)

    @pl.when(f == 0)
    def _():
        xb_s[...] = x_ref[...].astype(BF16)
        acc_s[...] = jnp.zeros_like(acc_s)

    @pl.when(tv_ref[j] > 0)
    def _():
        xb = xb_s[...]
        g = _dot(xb, wg_ref[0])
        u = _dot(xb, wu_ref[0])
        acc_s[...] += _dot((jax.nn.silu(g) * u).astype(BF16), wd_ref[0])

    @pl.when(f == pl.num_programs(1) - 1)
    def _():
        y_ref[...] = acc_s[...]


def _b6(te, tv, xg, w_gu, w_d, *, tm, tf):
    rg, d = xg.shape
    eff = w_d.shape[1]
    nf = eff // tf
    nt = rg // tm
    return pl.pallas_call(
        _b6_body,
        grid_spec=pltpu.PrefetchScalarGridSpec(
            num_scalar_prefetch=2,
            grid=(nt, nf),
            in_specs=[pl.BlockSpec((tm, d), lambda j, f, te, tv: (j, 0)),
                      pl.BlockSpec((1, d, tf), lambda j, f, te, tv: (te[j], 0, f)),
                      pl.BlockSpec((1, d, tf), lambda j, f, te, tv: (te[j], 0, nf + f)),
                      pl.BlockSpec((1, tf, d), lambda j, f, te, tv: (te[j], f, 0))],
            out_specs=pl.BlockSpec((tm, d), lambda j, f, te, tv: (j, 0)),
            scratch_shapes=[pltpu.VMEM((tm, d), BF16), pltpu.VMEM((tm, d), F32)]),
        out_shape=jax.ShapeDtypeStruct((rg, d), F32),
        compiler_params=_cparams(("arbitrary", "arbitrary")),
        name="b6_moe_ffn",
    )(te, tv, xg, w_gu, w_gu, w_d)


def _b7_body(off_ref, meta_ref, route_ref, h_ref, post_ref, yg_ref, out_ref, buf_s, sem, *, tm):
    def copy(r, k):
        pos = off_ref[meta_ref[0, k, r]] + meta_ref[0, 2 + k, r]
        return pltpu.make_async_copy(yg_ref.at[pl.ds(pos, 1)], buf_s.at[k, pl.ds(r, 1)], sem)

    def issue(r, _):
        copy(r, 0).start()
        copy(r, 1).start()
        return 0

    def drain(r, _):
        copy(r, 0).wait()
        copy(r, 1).wait()
        return 0

    lax.fori_loop(0, tm, issue, 0)
    lax.fori_loop(0, tm, drain, 0)
    route = route_ref[...]
    y = route[:, 0:1] * buf_s[0] + route[:, 1:2] * buf_s[1]
    out_ref[...] = h_ref[...] + _rms(y, post_ref[...])


def _b7(off, meta, route, h3, post, yg, *, tm):
    nblk = meta.shape[0]
    d = h3.shape[1]
    row = lambda i, off: (i, 0)
    return pl.pallas_call(
        functools.partial(_b7_body, tm=tm),
        grid_spec=pltpu.PrefetchScalarGridSpec(
            num_scalar_prefetch=1,
            grid=(nblk,),
            in_specs=[pl.BlockSpec((1, 8, tm), lambda i, off: (i, 0, 0), memory_space=pltpu.SMEM),
                      pl.BlockSpec((tm, LANES), row), pl.BlockSpec((tm, d), row),
                      pl.BlockSpec((1, d), lambda i, off: (0, 0)),
                      pl.BlockSpec(memory_space=pl.ANY)],
            out_specs=pl.BlockSpec((tm, d), row),
            scratch_shapes=[pltpu.VMEM((2, tm, d), F32), pltpu.SemaphoreType.DMA(())]),
        out_shape=jax.ShapeDtypeStruct((nblk * tm, d), F32),
        compiler_params=_cparams(("arbitrary",)),
        name="b7_combine",
    )(off, meta, route, h3, post, yg)


def _pick(n, target, mult):
    best = mult
    for cand in range(mult, min(n, target) + 1, mult):
        if n % cand == 0:
            best = cand
    return best


def kernel(x, meta_tokens, a_pre_norm, a_w_in, a_conv_w, a_conv_b, a_gate_a_w, a_gate_a_b, a_gate_x_w, a_gate_x_b, a_lambda, a_w_out, a_post_norm, kv_norm, kv_w_dkv, kv_latent_norm, kv_w_ukv, b_pre_norm, b_w_dq, b_q_norm, b_w_uq, b_w_o, b_post_norm, f_pre_norm, f_w_gate_up, f_w_down, f_post_norm, m_pre_norm, m_router_w, m_router_b, m_w_gate_up, m_w_down, m_post_norm):
    bsz, seq, d = x.shape
    n_meta = meta_tokens.shape[0]
    assert a_pre_norm.shape[0] == 1 and b_pre_norm.shape[0] == 1, "one RG-LRU layer + one MLA layer"
    t_all = n_meta + seq
    assert t_all % TT == 0 and seq % ATT_BLK == 0 and bsz % 8 == 0 and n_meta % 16 == 0
    rows = t_all * bsz
    row = lambda v: v.reshape(1, -1).astype(F32)
    bf = lambda w: w.astype(BF16)

    h0 = jnp.concatenate([jnp.broadcast_to(meta_tokens.astype(F32)[:, None, :], (n_meta, bsz, d)),
                          jnp.transpose(x, (1, 0, 2))], axis=0).reshape(rows, d)

    y, xr = _a1(h0, row(a_pre_norm[0]), bf(a_w_in[0]), tm=TT * bsz)
    wg = bf(jnp.concatenate([a_gate_a_w[0], a_gate_x_w[0]], axis=-1))
    h1 = _a2(xr, y, h0, a_conv_w[0].astype(F32), row(a_conv_b[0]), wg, row(a_gate_a_b[0]), row(a_gate_x_b[0]),
             row(a_lambda[0]), bf(a_w_out[0]), row(a_post_norm[0]), tt=TT_SMALL, bsz=bsz)
    d_ff = f_w_down.shape[1]
    h2 = _a3(h1, row(f_pre_norm[0]), bf(f_w_gate_up[0]), bf(f_w_down[0]), row(f_post_norm[0]),
             bsz=bsz, tt=TT, npad=PAD_T // TT, tf=_pick(d_ff, 1024, LANES))
    t_pad = PAD_T + t_all
    seq0 = PAD_T + n_meta
    assert seq0 % ATT_BLK == 0 and t_pad % (TT_SMALL * bsz) == 0

    kv_lora = kv_latent_norm.shape[0]
    nh = b_w_uq.shape[2] // (QK_NOPE + QK_ROPE)
    hr = QK_ROPE // 2
    scale = (QK_NOPE + QK_ROPE) ** -0.5 * math.log2(math.e)
    inv = ROPE_THETA ** (-jnp.arange(0, QK_ROPE, 2, dtype=F32) / QK_ROPE)
    pos = jnp.maximum(jnp.arange(t_pad, dtype=F32) - PAD_T, 0.0)
    ang = pos[:, None] * inv[None, :]
    cos, sin = jnp.cos(ang), jnp.sin(ang)
    one = jnp.ones((t_pad, QK_NOPE), F32)
    zero = jnp.zeros((t_pad, QK_NOPE), F32)
    tq = scale * jnp.concatenate([one, cos, cos, -sin, sin], axis=1)
    tkc = jnp.concatenate([zero, cos, cos, cos, cos], axis=1)
    tks = jnp.concatenate([zero, -sin, sin, -sin, sin], axis=1)

    w_c = kv_w_dkv[:, :kv_lora]
    k1 = kv_w_dkv[:, kv_lora:kv_lora + hr]
    k2 = kv_w_dkv[:, kv_lora + hr:]
    zk = jnp.zeros((d, QK_NOPE), F32)
    wka = jnp.concatenate([zk, k1, k2, k1, k2], axis=1)
    wkb = jnp.concatenate([zk, k2, k1, k2, k1], axis=1)
    ukv = kv_w_ukv.reshape(kv_lora, nh, QK_NOPE + V_HEAD)
    wk = jnp.concatenate([ukv[:, :, :QK_NOPE], jnp.zeros((kv_lora, nh, HEAD_PAD - QK_NOPE), F32)],
                         axis=2).reshape(kv_lora, nh * HEAD_PAD)
    wv = ukv[:, :, QK_NOPE:].reshape(kv_lora, nh * V_HEAD)
    uq = b_w_uq[0].reshape(-1, nh, QK_NOPE + QK_ROPE)
    qn_, q1, q2 = uq[:, :, :QK_NOPE], uq[:, :, QK_NOPE:QK_NOPE + hr], uq[:, :, QK_NOPE + hr:]
    wq = jnp.concatenate([qn_, q1, q2, q2, q1], axis=2).reshape(-1, nh * HEAD_PAD)
    q, k, v = _b1(h2.reshape(bsz * t_pad, d), row(kv_norm), row(b_pre_norm[0]), bf(w_c), bf(wka), bf(wkb),
                  row(kv_latent_norm), bf(wk), bf(wv), bf(b_w_dq[0]), row(b_q_norm[0]), bf(wq), tq, tkc, tks,
                  tm=TT_SMALL * bsz, nh=nh)

    o = _b2(q.reshape(bsz, t_pad, nh * HEAD_PAD), k.reshape(bsz, t_pad, nh * HEAD_PAD),
            v.reshape(bsz, t_pad, nh * V_HEAD), nh=nh, meta0=PAD_T, seq0=seq0, seq=seq,
            qb=_pick(seq, ATT_QBLK, ATT_BLK))

    rows_seq = seq * bsz
    tm1 = ATT_BLK
    assert rows_seq % MOE_TILE == 0
    ne = m_router_w.shape[2]
    rw = bf(jnp.concatenate([m_router_w[0], jnp.zeros((d, LANES - ne), F32)], axis=1))
    rb = jnp.concatenate([m_router_b[0].astype(F32), jnp.full((LANES - ne,), -1e30, F32)]).reshape(1, LANES)
    h3, xn, route = _b3(o, h2, bf(b_w_o[0]), row(b_post_norm[0]), row(m_pre_norm[0]), rw, rb,
                        tm=tm1, blk0=seq0 // tm1, nper=seq // tm1)

    meta, cnt = _b4(route, tm=tm1)
    tmg = MOE_TILE
    counts = cnt[0, :ne]
    padded = ((counts + tmg - 1) // tmg) * tmg
    ends = jnp.cumsum(padded)
    off = (ends - padded).astype(I32)
    nt = (TOP_K * rows_seq) // tmg + ne
    tile_start = jnp.arange(nt, dtype=I32) * tmg
    tv = (tile_start < ends[-1]).astype(I32)
    last = jnp.maximum(ends[-1] // tmg - 1, 0)
    te_raw = jnp.sum((tile_start[:, None] >= ends[None, :]).astype(I32), axis=1)
    te = jnp.minimum(jnp.where(tv > 0, te_raw, te_raw[last]), ne - 1).astype(I32)

    xg = _b5(off, meta, xn, jnp.zeros((nt * tmg, d), F32), tm=tm1)
    e_ff = m_w_down.shape[2]
    yg = _b6(te, tv, xg, bf(m_w_gate_up[0]), bf(m_w_down[0]), tm=tmg, tf=_pick(e_ff, 512, LANES))
    h4 = _b7(off, meta, route, h3, row(m_post_norm[0]), yg, tm=tm1)
    return h4.reshape(bsz, seq, d)
```

```python
import functools
import math

import jax
import jax.numpy as jnp
from jax import lax
from jax.experimental import pallas as pl
from jax.experimental.pallas import tpu as pltpu

F32 = jnp.float32
BF16 = jnp.bfloat16
I32 = jnp.int32

RMS_EPS = 1e-6
RG_C = 8.0
ROPE_THETA = 10000.0
QK_NOPE = 64
QK_ROPE = 32
V_HEAD = 64
HEAD_PAD = 128
TOP_K = 2
LANES = 128

TT = 48
TT_SMALL = 24
PAD_T = 240
ATT_BLK = 256
ATT_QBLK = 1024
MOE_TILE = 512
MOE_FF_CHUNK = 1792
VMEM_LIMIT = 56 * 1024 * 1024


def _cparams(sem):
    return pltpu.CompilerParams(dimension_semantics=sem, vmem_limit_bytes=VMEM_LIMIT)


def _rms(x, g):
    ms = jnp.mean(x * x, axis=-1, keepdims=True)
    return x * lax.rsqrt(ms + RMS_EPS) * g


def _dot(a, b):
    return jnp.dot(a, b, preferred_element_type=F32)


def _dot_t(a, b):
    return lax.dot_general(a, b, (((1,), (1,)), ((), ())), preferred_element_type=F32)


def _a1_body(h_ref, g_ref, w_ref, y_ref, xr_ref, *, c):
    xn = _rms(h_ref[...], g_ref[...]).astype(BF16)
    y_ref[...] = _dot(xn, w_ref[:, :c]).astype(BF16)
    xr_ref[...] = _dot(xn, w_ref[:, c:]).astype(BF16)


def _a1(h0, gain, w_in, *, tm):
    r, d = h0.shape
    c = w_in.shape[1] // 2
    row = lambda i: (i, 0)
    fixed = lambda i: (0, 0)
    return pl.pallas_call(
        functools.partial(_a1_body, c=c),
        grid=(r // tm,),
        in_specs=[pl.BlockSpec((tm, d), row), pl.BlockSpec((1, d), fixed),
                  pl.BlockSpec((d, 2 * c), fixed)],
        out_specs=[pl.BlockSpec((tm, c), row), pl.BlockSpec((tm, c), row)],
        out_shape=[jax.ShapeDtypeStruct((r, c), BF16), jax.ShapeDtypeStruct((r, c), BF16)],
        compiler_params=_cparams(("arbitrary",)),
        name="a1_norm_win",
    )(h0, gain, w_in)


def _a2_body(xr_ref, y_ref, h_ref, cw_ref, cb_ref, wg_ref, gab_ref, gxb_ref, lam_ref,
             wo_ref, pn_ref, out_ref, ext_s, a_s, b_s, carry_s, *, nb, tt, bsz, cw):
    tm = tt * bsz
    tail = (cw - 1) * bsz

    @pl.when(pl.program_id(0) == 0)
    def _():
        ext_s[0:tail, :] = jnp.zeros((tail, ext_s.shape[1]), F32)
        carry_s[...] = jnp.zeros_like(carry_s)

    ext_s[tail:tail + tm, :] = xr_ref[...].astype(F32)
    xc = cb_ref[...] + ext_s[0:tm, :] * cw_ref[0:1, :]
    for k in range(1, cw):
        xc = xc + ext_s[k * bsz:k * bsz + tm, :] * cw_ref[k:k + 1, :]
    ext_s[0:tail, :] = ext_s[tm:tm + tail, :]

    neg_lam = -lam_ref[...]
    sp = jnp.maximum(neg_lam, 0.0) + jnp.log1p(jnp.exp(-jnp.abs(neg_lam)))
    xcb = xc.astype(BF16)
    for n in range(nb):
        sl = slice(n * LANES, (n + 1) * LANES)
        g = _dot(xcb[:, sl], wg_ref[n])
        r = jax.nn.sigmoid(g[:, :LANES] + gab_ref[:, sl])
        ig = jax.nn.sigmoid(g[:, LANES:] + gxb_ref[:, sl])
        log_a = (-RG_C) * r * sp[:, sl]
        a = jnp.exp(log_a)
        a_s[:, sl] = a
        b_s[:, sl] = jnp.sqrt(1.0 - a * a) * (ig * xc[:, sl])

    def step(t, h):
        rows = pl.ds(pl.multiple_of(t * bsz, bsz), bsz)
        h = a_s[rows, :] * h + b_s[rows, :]
        b_s[rows, :] = h
        return h

    carry_s[...] = lax.fori_loop(0, tt, step, carry_s[...], unroll=4)

    z = (jax.nn.gelu(y_ref[...].astype(F32)) * b_s[...]).astype(BF16)
    mix = _dot(z, wo_ref[...])
    out_ref[...] = h_ref[...] + _rms(mix, pn_ref[...])


def _a2(xr, y, h0, conv_w, conv_b, wg, ga_b, gx_b, lam, w_out, post, *, tt, bsz):
    r, c = xr.shape
    d = h0.shape[1]
    nb = wg.shape[0]
    cw = conv_w.shape[0]
    tm = tt * bsz
    row = lambda i: (i, 0)
    fixed2 = lambda i: (0, 0)
    fixed3 = lambda i: (0, 0, 0)
    return pl.pallas_call(
        functools.partial(_a2_body, nb=nb, tt=tt, bsz=bsz, cw=cw),
        grid=(r // tm,),
        in_specs=[pl.BlockSpec((tm, c), row), pl.BlockSpec((tm, c), row), pl.BlockSpec((tm, d), row),
                  pl.BlockSpec((cw, c), fixed2), pl.BlockSpec((1, c), fixed2),
                  pl.BlockSpec((nb, LANES, 2 * LANES), fixed3),
                  pl.BlockSpec((1, c), fixed2), pl.BlockSpec((1, c), fixed2), pl.BlockSpec((1, c), fixed2),
                  pl.BlockSpec((c, d), fixed2), pl.BlockSpec((1, d), fixed2)],
        out_specs=pl.BlockSpec((tm, d), row),
        out_shape=jax.ShapeDtypeStruct((r, d), F32),
        scratch_shapes=[pltpu.VMEM((tm + (cw - 1) * bsz, c), F32), pltpu.VMEM((tm, c), F32),
                        pltpu.VMEM((tm, c), F32), pltpu.VMEM((bsz, c), F32)],
        compiler_params=_cparams(("arbitrary",)),
        name="a2_rglru",
    )(xr, y, h0, conv_w, conv_b, wg, ga_b, gx_b, lam, w_out, post)


def _a3_body(h_ref, pre_ref, wg_ref, wu_ref, wd_ref, post_ref, out_ref, xn_s, acc_s, tr_s, *, npad, bsz, tt):
    i = pl.program_id(0)
    f = pl.program_id(1)
    last = f == pl.num_programs(1) - 1

    @pl.when((i < npad) & last)
    def _():
        out_ref[...] = jnp.zeros_like(out_ref)

    @pl.when(i >= npad)
    def _():
        @pl.when(f == 0)
        def _():
            xn_s[...] = _rms(h_ref[...], pre_ref[...]).astype(BF16)
            acc_s[...] = jnp.zeros_like(acc_s)

        xn = xn_s[...]
        g = _dot(xn, wg_ref[...])
        u = _dot(xn, wu_ref[...])
        acc_s[...] += _dot((jax.nn.silu(g) * u).astype(BF16), wd_ref[...])

        @pl.when(last)
        def _():
            res = h_ref[...] + _rms(acc_s[...], post_ref[...])
            for c in range(tr_s.shape[0]):
                tr_s[c] = res[:, c * LANES:(c + 1) * LANES]
            for b in range(bsz):
                for c in range(tr_s.shape[0]):
                    out_ref[b, :, c * LANES:(c + 1) * LANES] = tr_s[c, pl.ds(b, tt, stride=bsz), :]


def _a3(h1, pre, w_gu, w_d, post, *, bsz, tt, npad, tf):
    r, d = h1.shape
    tm = tt * bsz
    ff = w_d.shape[0]
    nf = ff // tf
    nblk = r // tm
    row = lambda i, f: (jnp.maximum(i - npad, 0), 0)
    fixed = lambda i, f: (0, 0)
    fsel = lambda i, f: jnp.where(i < npad, 0, f)
    return pl.pallas_call(
        functools.partial(_a3_body, npad=npad, bsz=bsz, tt=tt),
        grid=(npad + nblk, nf),
        in_specs=[pl.BlockSpec((tm, d), row), pl.BlockSpec((1, d), fixed),
                  pl.BlockSpec((d, tf), lambda i, f: (0, fsel(i, f))),
                  pl.BlockSpec((d, tf), lambda i, f: (0, nf + fsel(i, f))),
                  pl.BlockSpec((tf, d), lambda i, f: (fsel(i, f), 0)),
                  pl.BlockSpec((1, d), fixed)],
        out_specs=pl.BlockSpec((bsz, tt, d), lambda i, f: (0, i, 0)),
        out_shape=jax.ShapeDtypeStruct((bsz, (npad + nblk) * tt, d), F32),
        scratch_shapes=[pltpu.VMEM((tm, d), BF16), pltpu.VMEM((tm, d), F32),
                        pltpu.VMEM((d // LANES, tm, LANES), F32)],
        compiler_params=_cparams(("arbitrary", "arbitrary")),
        name="a3_dense_ffn",
    )(h1, pre, w_gu, w_gu, w_d, post)


def _b1_body(h_ref, kvn_ref, bpre_ref, wc_ref, wka_ref, wkb_ref, latn_ref, wk_ref, wv_ref,
             wdq_ref, qn_ref, wq_ref, tq_ref, tkc_ref, tks_ref, q_ref, k_ref, v_ref, *, nh):
    h = h_ref[...]
    hn = h * lax.rsqrt(jnp.mean(h * h, axis=-1, keepdims=True) + RMS_EPS)
    ukv = (hn * kvn_ref[...]).astype(BF16)
    uq = (hn * bpre_ref[...]).astype(BF16)

    ckv = _rms(_dot(ukv, wc_ref[...]), latn_ref[...]).astype(BF16)
    kr = _dot(ukv, wka_ref[...]) * tkc_ref[...] + _dot(ukv, wkb_ref[...]) * tks_ref[...]
    kn = _dot(ckv, wk_ref[...])
    for hd in range(nh):
        sl = slice(hd * HEAD_PAD, (hd + 1) * HEAD_PAD)
        k_ref[:, sl] = (kn[:, sl] + kr).astype(BF16)
    v_ref[...] = _dot(ckv, wv_ref[...]).astype(BF16)

    cq = _rms(_dot(uq, wdq_ref[...]), qn_ref[...]).astype(BF16)
    qr = _dot(cq, wq_ref[...])
    tq = tq_ref[...]
    for hd in range(nh):
        sl = slice(hd * HEAD_PAD, (hd + 1) * HEAD_PAD)
        q_ref[:, sl] = (qr[:, sl] * tq).astype(BF16)


def _b1(h2, kvn, bpre, wc, wka, wkb, latn, wk, wv, wdq, qn, wq, tq, tkc, tks, *, tm, nh):
    r, d = h2.shape
    nper = tq.shape[0] // tm
    row = lambda i: (i, 0)
    tab = lambda i: (i % nper, 0)
    fixed = lambda i: (0, 0)
    full = lambda a: pl.BlockSpec(a.shape, fixed)
    return pl.pallas_call(
        functools.partial(_b1_body, nh=nh),
        grid=(r // tm,),
        in_specs=[pl.BlockSpec((tm, d), row), full(kvn), full(bpre), full(wc), full(wka), full(wkb),
                  full(latn), full(wk), full(wv), full(wdq), full(qn), full(wq),
                  pl.BlockSpec((tm, HEAD_PAD), tab), pl.BlockSpec((tm, HEAD_PAD), tab),
                  pl.BlockSpec((tm, HEAD_PAD), tab)],
        out_specs=[pl.BlockSpec((tm, nh * HEAD_PAD), row), pl.BlockSpec((tm, nh * HEAD_PAD), row),
                   pl.BlockSpec((tm, nh * V_HEAD), row)],
        out_shape=[jax.ShapeDtypeStruct((r, nh * HEAD_PAD), BF16),
                   jax.ShapeDtypeStruct((r, nh * HEAD_PAD), BF16),
                   jax.ShapeDtypeStruct((r, nh * V_HEAD), BF16)],
        compiler_params=_cparams(("arbitrary",)),
        name="b1_qkv_proj",
    )(h2, kvn, bpre, wc, wka, wkb, latn, wk, wv, wdq, qn, wq, tq, tkc, tks)


def _tile2(x):
    return jnp.concatenate([x, x], axis=1)


def _b2_body(q_ref, k_ref, v_ref, o_ref, vext_s, m0_s, m1_s, acc0_s, acc1_s, *, meta0, seq0, nq, qb, kb):
    pair = 2 * V_HEAD
    t_pad = v_ref.shape[1]
    m_s = (m0_s, m1_s)
    acc_s = (acc0_s, acc1_s)
    vext_s[meta0:, 0:pair] = v_ref[0, meta0:, :]
    vext_s[meta0:, pair:] = jnp.ones((t_pad - meta0, pair), BF16)
    o_ref[0, 0:seq0, :] = jnp.zeros((seq0, pair), o_ref.dtype)
    lane = lax.broadcasted_iota(I32, (qb, pair), 1)
    neg = jnp.finfo(F32).min

    def update(hh, qh, qlo, krows, masked):
        s = _dot_t(qh[qlo:], k_ref[0, krows, hh * HEAD_PAD:(hh + 1) * HEAD_PAD])
        if masked:
            rowi = lax.broadcasted_iota(I32, s.shape, 0)
            coli = lax.broadcasted_iota(I32, s.shape, 1)
            s = jnp.where(coli <= rowi, s, neg)
        m_old = m_s[hh][qlo:, :]
        m_new = jnp.maximum(m_old, jnp.max(s, axis=-1, keepdims=True))
        alpha = jnp.exp2(m_old - m_new)
        p = jnp.exp2(s - _tile2(m_new))
        acc_s[hh][qlo:, :] = _tile2(alpha) * acc_s[hh][qlo:, :] + _dot(p.astype(BF16), vext_s[krows, :])
        m_s[hh][qlo:, :] = m_new

    def qblock(i, _):
        q0 = pl.multiple_of(seq0 + i * qb, kb)
        qs = [q_ref[0, pl.ds(q0, qb), hh * HEAD_PAD:(hh + 1) * HEAD_PAD] for hh in range(2)]
        for hh in range(2):
            s = _dot_t(qs[hh], k_ref[0, meta0:seq0, hh * HEAD_PAD:(hh + 1) * HEAD_PAD])
            m = jnp.max(s, axis=-1, keepdims=True)
            p = jnp.exp2(s - m)
            acc_s[hh][...] = _dot(p.astype(BF16), vext_s[meta0:seq0, :])
            m_s[hh][...] = jnp.broadcast_to(m, (qb, pair))

        def kv_step(j, _):
            krows = pl.ds(pl.multiple_of(seq0 + j * kb, kb), kb)
            for hh in range(2):
                update(hh, qs[hh], 0, krows, False)
            return 0

        lax.fori_loop(0, i * (qb // kb), kv_step, 0)
        for jj in range(qb // kb):
            krows = pl.ds(pl.multiple_of(q0 + jj * kb, kb), kb)
            for hh in range(2):
                update(hh, qs[hh], jj * kb, krows, True)
        res = []
        for hh in range(2):
            acc = acc_s[hh][...]
            res.append(acc[:, :pair] / acc[:, pair:])
        o_ref[0, pl.ds(q0, qb), :] = jnp.where(lane < V_HEAD, res[0], res[1]).astype(o_ref.dtype)
        return 0

    lax.fori_loop(0, nq, qblock, 0)


def _b2(q, k, v, *, nh, meta0, seq0, seq, qb):
    bsz, t_pad, _ = q.shape
    npair = nh // 2
    qk = pl.BlockSpec((1, t_pad, 2 * HEAD_PAD), lambda b, p: (b, 0, p))
    vo = pl.BlockSpec((1, t_pad, 2 * V_HEAD), lambda b, p: (b, 0, p))
    return pl.pallas_call(
        functools.partial(_b2_body, meta0=meta0, seq0=seq0, nq=seq // qb, qb=qb, kb=ATT_BLK),
        grid=(bsz, npair),
        in_specs=[qk, qk, vo],
        out_specs=vo,
        out_shape=jax.ShapeDtypeStruct(v.shape, BF16),
        scratch_shapes=[pltpu.VMEM((t_pad, 4 * V_HEAD), BF16),
                        pltpu.VMEM((qb, 2 * V_HEAD), F32), pltpu.VMEM((qb, 2 * V_HEAD), F32),
                        pltpu.VMEM((qb, 4 * V_HEAD), F32), pltpu.VMEM((qb, 4 * V_HEAD), F32)],
        compiler_params=_cparams(("arbitrary", "arbitrary")),
        name="b2_attention",
    )(q, k, v)


def _b3_body(o_ref, h_ref, wo_ref, post_ref, mpre_ref, rw_ref, rb_ref, h3_ref, xn_ref, route_ref):
    mix = _dot(o_ref[0], wo_ref[...])
    h3 = h_ref[0] + _rms(mix, post_ref[...])
    h3_ref[...] = h3
    xn = _rms(h3, mpre_ref[...])
    xn_ref[...] = xn
    logits = _dot(xn.astype(BF16), rw_ref[...]) + rb_ref[...]
    lane = lax.broadcasted_iota(I32, logits.shape, 1).astype(F32)
    big = float(LANES)
    m1 = jnp.max(logits, axis=-1, keepdims=True)
    i1 = jnp.min(jnp.where(logits == m1, lane, big), axis=-1, keepdims=True)
    rest = jnp.where(lane == i1, -jnp.inf, logits)
    m2 = jnp.max(rest, axis=-1, keepdims=True)
    i2 = jnp.min(jnp.where(rest == m2, lane, big), axis=-1, keepdims=True)
    e2 = jnp.exp(m2 - m1)
    g1 = 1.0 / (1.0 + e2)
    g2 = e2 / (1.0 + e2)
    route_ref[...] = jnp.where(lane == 0.0, g1, jnp.where(lane == 1.0, g2,
                               jnp.where(lane == 2.0, i1, jnp.where(lane == 3.0, i2, 0.0))))


def _b3(o, h2, w_o, post, mpre, rw, rb, *, tm, blk0, nper):
    bsz, _, d = h2.shape
    src = lambda i: (i // nper, blk0 + i % nper, 0)
    dst = lambda i: (i, 0)
    fixed = lambda i: (0, 0)
    nblk = bsz * nper
    rows = nblk * tm
    return pl.pallas_call(
        _b3_body,
        grid=(nblk,),
        in_specs=[pl.BlockSpec((1, tm, o.shape[2]), src), pl.BlockSpec((1, tm, d), src),
                  pl.BlockSpec(w_o.shape, fixed), pl.BlockSpec((1, d), fixed), pl.BlockSpec((1, d), fixed),
                  pl.BlockSpec(rw.shape, fixed), pl.BlockSpec((1, LANES), fixed)],
        out_specs=[pl.BlockSpec((tm, d), dst), pl.BlockSpec((tm, d), dst), pl.BlockSpec((tm, LANES), dst)],
        out_shape=[jax.ShapeDtypeStruct((rows, d), F32), jax.ShapeDtypeStruct((rows, d), F32),
                   jax.ShapeDtypeStruct((rows, LANES), F32)],
        compiler_params=_cparams(("arbitrary",)),
        name="b3_oproj_router",
    )(o, h2, w_o, post, mpre, rw, rb)


def _b4_body(route_ref, meta_ref, cnt_ref, run_s, off_s, *, tile):
    phase = pl.program_id(0)
    i = pl.program_id(1)
    route = route_ref[...]
    tm = route.shape[0]
    lane = lax.broadcasted_iota(I32, route.shape, 1).astype(F32)
    i1 = jnp.sum(jnp.where(lane == 2.0, route, 0.0), axis=-1, keepdims=True)
    i2 = jnp.sum(jnp.where(lane == 3.0, route, 0.0), axis=-1, keepdims=True)
    oh1 = lane == i1
    oh2 = lane == i2
    onehot = jnp.where(oh1 | oh2, 1.0, 0.0)
    colsum = jnp.sum(onehot, axis=0, keepdims=True)

    @pl.when((phase == 0) & (i == 0))
    def _():
        run_s[...] = jnp.zeros_like(run_s)

    @pl.when(phase == 0)
    def _():
        run_s[...] += colsum

    @pl.when((phase == 1) & (i == 0))
    def _():
        total = run_s[...]
        cnt_ref[...] = jnp.broadcast_to(total, cnt_ref.shape).astype(I32)
        ntiles = jnp.floor((total + (tile - 1)) * (1.0 / tile))
        rr = lax.broadcasted_iota(I32, (LANES, LANES), 0)
        cc = lax.broadcasted_iota(I32, (LANES, LANES), 1)
        upper = jnp.where(rr < cc, 1.0, 0.0).astype(BF16)
        before = _dot(jnp.broadcast_to(ntiles, (8, LANES)).astype(BF16), upper)
        off_s[...] = before[0:1, :] * float(tile)
        run_s[...] = jnp.zeros_like(run_s)

    @pl.when(phase == 1)
    def _():
        rr = lax.broadcasted_iota(I32, (tm, tm), 0)
        cc = lax.broadcasted_iota(I32, (tm, tm), 1)
        lower = jnp.where(cc < rr, 1.0, 0.0).astype(BF16)
        run = run_s[...]
        pos = _dot(lower, onehot.astype(BF16)) + (run + off_s[...])
        p1 = jnp.sum(jnp.where(oh1, pos, 0.0), axis=-1, keepdims=True)
        p2 = jnp.sum(jnp.where(oh2, pos, 0.0), axis=-1, keepdims=True)
        packed = jnp.where(lane == 0.0, p1, jnp.where(lane == 1.0, p2, 0.0))
        meta_ref[0] = jnp.transpose(packed)[0:8, :].astype(I32)
        run_s[...] = run + colsum


def _b4(route, *, tm, tile):
    nblk = route.shape[0] // tm
    return pl.pallas_call(
        functools.partial(_b4_body, tile=tile),
        grid=(2, nblk),
        in_specs=[pl.BlockSpec((tm, LANES), lambda p, i: (i, 0))],
        out_specs=[pl.BlockSpec((1, 8, tm), lambda p, i: (i * p, 0, 0)),
                   pl.BlockSpec((8, LANES), lambda p, i: (0, 0))],
        out_shape=[jax.ShapeDtypeStruct((nblk, 8, tm), I32), jax.ShapeDtypeStruct((8, LANES), I32)],
        scratch_shapes=[pltpu.VMEM((1, LANES), F32), pltpu.VMEM((1, LANES), F32)],
        compiler_params=_cparams(("arbitrary", "arbitrary")),
        name="b4_route_rank",
    )(route)


def _b5_body(meta_ref, xn_ref, xg_in_ref, xg_ref, sem, *, tm):
    del xg_in_ref

    def issue(r, _):
        for k in range(TOP_K):
            pltpu.make_async_copy(xn_ref.at[pl.ds(r, 1)], xg_ref.at[pl.ds(meta_ref[0, k, r], 1)], sem).start()
        return 0

    lax.fori_loop(0, tm, issue, 0, unroll=8)
    for k in range(TOP_K):
        pltpu.make_async_copy(xn_ref, xg_ref.at[pl.ds(0, tm)], sem).wait()


def _b5(meta, xn, xg0, *, tm):
    nblk = meta.shape[0]
    return pl.pallas_call(
        functools.partial(_b5_body, tm=tm),
        grid=(nblk,),
        in_specs=[pl.BlockSpec((1, 8, tm), lambda i: (i, 0, 0), memory_space=pltpu.SMEM),
                  pl.BlockSpec((tm, xn.shape[1]), lambda i: (i, 0)),
                  pl.BlockSpec(memory_space=pl.ANY)],
        out_specs=pl.BlockSpec(memory_space=pl.ANY),
        scratch_shapes=[pltpu.SemaphoreType.DMA(())],
        out_shape=jax.ShapeDtypeStruct(xg0.shape, xg0.dtype),
        input_output_aliases={2: 0},
        compiler_params=_cparams(("arbitrary",)),
        name="b5_dispatch",
    )(meta, xn, xg0)


def _b6_body(te_ref, tv_ref, x_ref, wg_ref, wu_ref, wd_ref, y_ref, xb_s, acc_s):
    j = pl.program_id(0)
    f = pl.program_id(1)

    @pl.when(f == 0)
    def _():
        xb_s[...] = x_ref[...].astype(BF16)
        acc_s[...] = jnp.zeros_like(acc_s)

    @pl.when(tv_ref[j] > 0)
    def _():
        xb = xb_s[...]
        g = _dot(xb, wg_ref[0])
        u = _dot(xb, wu_ref[0])
        acc_s[...] += _dot((jax.nn.silu(g) * u).astype(BF16), wd_ref[0])

    @pl.when(f == pl.num_programs(1) - 1)
    def _():
        y_ref[...] = acc_s[...]


def _b6(te, tv, xg, w_gu, w_d, *, tm, tf):
    rg, d = xg.shape
    eff = w_d.shape[1]
    nf = eff // tf
    nt = rg // tm
    return pl.pallas_call(
        _b6_body,
        grid_spec=pltpu.PrefetchScalarGridSpec(
            num_scalar_prefetch=2,
            grid=(nt, nf),
            in_specs=[pl.BlockSpec((tm, d), lambda j, f, te, tv: (j, 0)),
                      pl.BlockSpec((1, d, tf), lambda j, f, te, tv: (te[j], 0, f)),
                      pl.BlockSpec((1, d, tf), lambda j, f, te, tv: (te[j], 0, nf + f)),
                      pl.BlockSpec((1, tf, d), lambda j, f, te, tv: (te[j], f, 0))],
            out_specs=pl.BlockSpec((tm, d), lambda j, f, te, tv: (j, 0)),
            scratch_shapes=[pltpu.VMEM((tm, d), BF16), pltpu.VMEM((tm, d), F32)]),
        out_shape=jax.ShapeDtypeStruct((rg, d), F32),
        compiler_params=_cparams(("arbitrary", "arbitrary")),
        name="b6_moe_ffn",
    )(te, tv, xg, w_gu, w_gu, w_d)


def _b7_body(meta_ref, meta_next_ref, route_ref, h_ref, post_ref, yg_ref, out_ref, buf_s, sem, *, tm):
    i = pl.program_id(0)
    slot = i % 2

    def gather_tile(mref, s):
        def issue(r, _):
            for k in range(TOP_K):
                pltpu.make_async_copy(yg_ref.at[pl.ds(mref[0, k, r], 1)], buf_s.at[s, k, pl.ds(r, 1)],
                                      sem.at[s]).start()
            return 0

        lax.fori_loop(0, tm, issue, 0, unroll=8)

    @pl.when(i == 0)
    def _():
        gather_tile(meta_ref, 0)

    @pl.when(i + 1 < pl.num_programs(0))
    def _():
        gather_tile(meta_next_ref, 1 - slot)

    for k in range(TOP_K):
        pltpu.make_async_copy(yg_ref.at[pl.ds(0, tm)], buf_s.at[slot, k], sem.at[slot]).wait()
    route = route_ref[...]
    y = route[:, 0:1] * buf_s[slot, 0] + route[:, 1:2] * buf_s[slot, 1]
    out_ref[...] = h_ref[...] + _rms(y, post_ref[...])


def _b7(meta, route, h3, post, yg, *, tm):
    nblk = meta.shape[0]
    d = h3.shape[1]
    row = lambda i: (i, 0)
    return pl.pallas_call(
        functools.partial(_b7_body, tm=tm),
        grid=(nblk,),
        in_specs=[pl.BlockSpec((1, 8, tm), lambda i: (i, 0, 0), memory_space=pltpu.SMEM),
                  pl.BlockSpec((1, 8, tm), lambda i: (jnp.minimum(i + 1, nblk - 1), 0, 0),
                               memory_space=pltpu.SMEM),
                  pl.BlockSpec((tm, LANES), row), pl.BlockSpec((tm, d), row),
                  pl.BlockSpec((1, d), lambda i: (0, 0)),
                  pl.BlockSpec(memory_space=pl.ANY)],
        out_specs=pl.BlockSpec((tm, d), row),
        scratch_shapes=[pltpu.VMEM((2, TOP_K, tm, d), F32), pltpu.SemaphoreType.DMA((2,))],
        out_shape=jax.ShapeDtypeStruct((nblk * tm, d), F32),
        compiler_params=_cparams(("arbitrary",)),
        name="b7_combine",
    )(meta, meta, route, h3, post, yg)


def _pick(n, target, mult):
    best = mult
    for cand in range(mult, min(n, target) + 1, mult):
        if n % cand == 0:
            best = cand
    return best


def kernel(x, meta_tokens, a_pre_norm, a_w_in, a_conv_w, a_conv_b, a_gate_a_w, a_gate_a_b, a_gate_x_w, a_gate_x_b, a_lambda, a_w_out, a_post_norm, kv_norm, kv_w_dkv, kv_latent_norm, kv_w_ukv, b_pre_norm, b_w_dq, b_q_norm, b_w_uq, b_w_o, b_post_norm, f_pre_norm, f_w_gate_up, f_w_down, f_post_norm, m_pre_norm, m_router_w, m_router_b, m_w_gate_up, m_w_down, m_post_norm):
    bsz, seq, d = x.shape
    n_meta = meta_tokens.shape[0]
    assert a_pre_norm.shape[0] == 1 and b_pre_norm.shape[0] == 1, "one RG-LRU layer + one MLA layer"
    t_all = n_meta + seq
    assert t_all % TT == 0 and seq % ATT_BLK == 0 and bsz % 8 == 0 and n_meta % 16 == 0
    rows = t_all * bsz
    row = lambda v: v.reshape(1, -1).astype(F32)
    bf = lambda w: w.astype(BF16)

    h0 = jnp.concatenate([jnp.broadcast_to(meta_tokens.astype(F32)[:, None, :], (n_meta, bsz, d)),
                          jnp.transpose(x, (1, 0, 2))], axis=0).reshape(rows, d)

    y, xr = _a1(h0, row(a_pre_norm[0]), bf(a_w_in[0]), tm=TT * bsz)
    wg = bf(jnp.concatenate([a_gate_a_w[0], a_gate_x_w[0]], axis=-1))
    h1 = _a2(xr, y, h0, a_conv_w[0].astype(F32), row(a_conv_b[0]), wg, row(a_gate_a_b[0]), row(a_gate_x_b[0]),
             row(a_lambda[0]), bf(a_w_out[0]), row(a_post_norm[0]), tt=TT_SMALL, bsz=bsz)
    d_ff = f_w_down.shape[1]
    h2 = _a3(h1, row(f_pre_norm[0]), bf(f_w_gate_up[0]), bf(f_w_down[0]), row(f_post_norm[0]),
             bsz=bsz, tt=TT, npad=PAD_T // TT, tf=_pick(d_ff, 1024, LANES))
    t_pad = PAD_T + t_all
    seq0 = PAD_T + n_meta
    assert seq0 % ATT_BLK == 0 and t_pad % (TT_SMALL * bsz) == 0

    kv_lora = kv_latent_norm.shape[0]
    nh = b_w_uq.shape[2] // (QK_NOPE + QK_ROPE)
    hr = QK_ROPE // 2
    scale = (QK_NOPE + QK_ROPE) ** -0.5 * math.log2(math.e)
    inv = ROPE_THETA ** (-jnp.arange(0, QK_ROPE, 2, dtype=F32) / QK_ROPE)
    pos = jnp.maximum(jnp.arange(t_pad, dtype=F32) - PAD_T, 0.0)
    ang = pos[:, None] * inv[None, :]
    cos, sin = jnp.cos(ang), jnp.sin(ang)
    one = jnp.ones((t_pad, QK_NOPE), F32)
    zero = jnp.zeros((t_pad, QK_NOPE), F32)
    tq = scale * jnp.concatenate([one, cos, cos, -sin, sin], axis=1)
    tkc = jnp.concatenate([zero, cos, cos, cos, cos], axis=1)
    tks = jnp.concatenate([zero, -sin, sin, -sin, sin], axis=1)

    w_c = kv_w_dkv[:, :kv_lora]
    k1 = kv_w_dkv[:, kv_lora:kv_lora + hr]
    k2 = kv_w_dkv[:, kv_lora + hr:]
    zk = jnp.zeros((d, QK_NOPE), F32)
    wka = jnp.concatenate([zk, k1, k2, k1, k2], axis=1)
    wkb = jnp.concatenate([zk, k2, k1, k2, k1], axis=1)
    ukv = kv_w_ukv.reshape(kv_lora, nh, QK_NOPE + V_HEAD)
    wk = jnp.concatenate([ukv[:, :, :QK_NOPE], jnp.zeros((kv_lora, nh, HEAD_PAD - QK_NOPE), F32)],
                         axis=2).reshape(kv_lora, nh * HEAD_PAD)
    wv = ukv[:, :, QK_NOPE:].reshape(kv_lora, nh * V_HEAD)
    uq = b_w_uq[0].reshape(-1, nh, QK_NOPE + QK_ROPE)
    qn_, q1, q2 = uq[:, :, :QK_NOPE], uq[:, :, QK_NOPE:QK_NOPE + hr], uq[:, :, QK_NOPE + hr:]
    wq = jnp.concatenate([qn_, q1, q2, q2, q1], axis=2).reshape(-1, nh * HEAD_PAD)
    q, k, v = _b1(h2.reshape(bsz * t_pad, d), row(kv_norm), row(b_pre_norm[0]), bf(w_c), bf(wka), bf(wkb),
                  row(kv_latent_norm), bf(wk), bf(wv), bf(b_w_dq[0]), row(b_q_norm[0]), bf(wq), tq, tkc, tks,
                  tm=TT_SMALL * bsz, nh=nh)

    o = _b2(q.reshape(bsz, t_pad, nh * HEAD_PAD), k.reshape(bsz, t_pad, nh * HEAD_PAD),
            v.reshape(bsz, t_pad, nh * V_HEAD), nh=nh, meta0=PAD_T, seq0=seq0, seq=seq,
            qb=_pick(seq, ATT_QBLK, ATT_BLK))

    rows_seq = seq * bsz
    tm1 = ATT_BLK
    assert rows_seq % MOE_TILE == 0
    ne = m_router_w.shape[2]
    rw = bf(jnp.concatenate([m_router_w[0], jnp.zeros((d, LANES - ne), F32)], axis=1))
    rb = jnp.concatenate([m_router_b[0].astype(F32), jnp.full((LANES - ne,), -1e30, F32)]).reshape(1, LANES)
    h3, xn, route = _b3(o, h2, bf(b_w_o[0]), row(b_post_norm[0]), row(m_pre_norm[0]), rw, rb,
                        tm=tm1, blk0=seq0 // tm1, nper=seq // tm1)

    tmg = MOE_TILE
    meta, cnt = _b4(route, tm=tm1, tile=tmg)
    counts = cnt[0, :ne]
    padded = ((counts + tmg - 1) // tmg) * tmg
    ends = jnp.cumsum(padded)
    nt = (TOP_K * rows_seq) // tmg + ne
    tile_start = jnp.arange(nt, dtype=I32) * tmg
    tv = (tile_start < ends[-1]).astype(I32)
    last = jnp.maximum(ends[-1] // tmg - 1, 0)
    te_raw = jnp.sum((tile_start[:, None] >= ends[None, :]).astype(I32), axis=1)
    te = jnp.minimum(jnp.where(tv > 0, te_raw, te_raw[last]), ne - 1).astype(I32)

    xg = _b5(meta, xn, jnp.zeros((nt * tmg, d), F32), tm=tm1)
    e_ff = m_w_down.shape[2]
    yg = _b6(te, tv, xg, bf(m_w_gate_up[0]), bf(m_w_down[0]), tm=tmg, tf=_pick(e_ff, MOE_FF_CHUNK, LANES))
    h4 = _b7(meta, route, h3, row(m_post_norm[0]), yg, tm=tm1)
    return h4.reshape(bsz, seq, d)
```

```python
import functools
import math

import jax
import jax.numpy as jnp
from jax import lax
from jax.experimental import pallas as pl
from jax.experimental.pallas import tpu as pltpu

F32 = jnp.float32
BF16 = jnp.bfloat16
I32 = jnp.int32

RMS_EPS = 1e-6
RG_C = 8.0
ROPE_THETA = 10000.0
QK_NOPE = 64
QK_ROPE = 32
V_HEAD = 64
HEAD_PAD = 128
TOP_K = 2
LANES = 128
SUBLANES = 8

TT = 48
TT_SMALL = 24
PAD_T = 240
ATT_BLK = 256
ATT_QBLK = 1024
MOE_TILE = 512
MOE_FF_CHUNK = 1792
VMEM_LIMIT = 56 * 1024 * 1024


def _cparams(sem):
    return pltpu.CompilerParams(dimension_semantics=sem, vmem_limit_bytes=VMEM_LIMIT)


def _rms(x, g):
    ms = jnp.mean(x * x, axis=-1, keepdims=True)
    return x * lax.rsqrt(ms + RMS_EPS) * g


def _sigmoid(x):
    return 0.5 * jnp.tanh(0.5 * x) + 0.5


def _dot(a, b):
    return jnp.dot(a, b, preferred_element_type=F32)


def _dot_t(a, b):
    return lax.dot_general(a, b, (((1,), (1,)), ((), ())), preferred_element_type=F32)


def _a2_body(h_ref, hnext_ref, pre_ref, win_ref, cw_ref, cb_ref, wg_ref, gab_ref, gxb_ref, lam_ref,
             wo_ref, pn_ref, out_ref, yx_s, ext_s, a_s, b_s, carry_s, *, nb, tt, bsz, cw):
    tm = tt * bsz
    tail = (cw - 1) * bsz
    c = ext_s.shape[1]
    i = pl.program_id(0)
    slot = i % 2

    def project(href, s):
        xn = _rms(href[...], pre_ref[...]).astype(BF16)
        yx_s[s, :, 0:c] = _dot(xn, win_ref[:, 0:c]).astype(BF16)
        yx_s[s, :, c:] = _dot(xn, win_ref[:, c:]).astype(BF16)

    @pl.when(i == 0)
    def _():
        ext_s[0:tail, :] = jnp.zeros((tail, c), F32)
        carry_s[...] = jnp.zeros_like(carry_s)
        project(h_ref, 0)

    ext_s[tail:tail + tm, :] = yx_s[slot, :, c:].astype(F32)

    xn_next = _rms(hnext_ref[...], pre_ref[...]).astype(BF16)
    pw = 2 * LANES
    pieces = [(s0, min(s0 + pw, 2 * c)) for s0 in range(0, 2 * c, pw)]

    def project_piece(p):
        lo, hi = pieces[p]
        yx_s[1 - slot, :, lo:hi] = _dot(xn_next, win_ref[:, lo:hi]).astype(BF16)

    xc = cb_ref[...] + ext_s[0:tm, :] * cw_ref[0:1, :]
    for k in range(1, cw):
        xc = xc + ext_s[k * bsz:k * bsz + tm, :] * cw_ref[k:k + 1, :]
    ext_s[0:tail, :] = ext_s[tm:tm + tail, :]

    neg_lam = -lam_ref[...]
    sp = jnp.maximum(neg_lam, 0.0) + jnp.log1p(jnp.exp(-jnp.abs(neg_lam)))
    rate = sp * (-RG_C * math.log2(math.e))
    xcb = xc.astype(BF16)
    per_block = -(-len(pieces) // nb)
    for n in range(nb):
        for p in range(n * per_block, min((n + 1) * per_block, len(pieces))):
            project_piece(p)
        sl = slice(n * LANES, (n + 1) * LANES)
        g = _dot(xcb[:, sl], wg_ref[n])
        r = _sigmoid(g[:, :LANES] + gab_ref[:, sl])
        ig = _sigmoid(g[:, LANES:] + gxb_ref[:, sl])
        a = jnp.exp2(r * rate[:, sl])
        a_s[:, sl] = a
        u = 1.0 - a * a
        root = jnp.where(u > 0.0, u * lax.rsqrt(u), 0.0)
        b_s[:, sl] = root * (ig * xc[:, sl])

    def step(t, h):
        rows = pl.ds(pl.multiple_of(t * bsz, bsz), bsz)
        h = a_s[rows, :] * h + b_s[rows, :]
        b_s[rows, :] = h
        return h

    carry_s[...] = lax.fori_loop(0, tt, step, carry_s[...], unroll=4)

    z = (jax.nn.gelu(yx_s[slot, :, 0:c].astype(F32)) * b_s[...]).astype(BF16)
    mix = _dot(z, wo_ref[...])
    out_ref[...] = h_ref[...] + _rms(mix, pn_ref[...])


def _a2(h0, pre, w_in, conv_w, conv_b, wg, ga_b, gx_b, lam, w_out, post, *, tt, bsz):
    r, d = h0.shape
    c = w_in.shape[1] // 2
    nb = wg.shape[0]
    cw = conv_w.shape[0]
    tm = tt * bsz
    nblk = r // tm
    row = lambda i: (i, 0)
    fixed2 = lambda i: (0, 0)
    fixed3 = lambda i: (0, 0, 0)
    return pl.pallas_call(
        functools.partial(_a2_body, nb=nb, tt=tt, bsz=bsz, cw=cw),
        grid=(nblk,),
        in_specs=[pl.BlockSpec((tm, d), row),
                  pl.BlockSpec((tm, d), lambda i: (jnp.minimum(i + 1, nblk - 1), 0)),
                  pl.BlockSpec((1, d), fixed2),
                  pl.BlockSpec((d, 2 * c), fixed2, pipeline_mode=pl.Buffered(1)),
                  pl.BlockSpec((cw, c), fixed2), pl.BlockSpec((1, c), fixed2),
                  pl.BlockSpec((nb, LANES, 2 * LANES), fixed3),
                  pl.BlockSpec((1, c), fixed2), pl.BlockSpec((1, c), fixed2), pl.BlockSpec((1, c), fixed2),
                  pl.BlockSpec((c, d), fixed2, pipeline_mode=pl.Buffered(1)), pl.BlockSpec((1, d), fixed2)],
        out_specs=pl.BlockSpec((tm, d), row),
        out_shape=jax.ShapeDtypeStruct((r, d), F32),
        scratch_shapes=[pltpu.VMEM((2, tm, 2 * c), BF16),
                        pltpu.VMEM((tm + (cw - 1) * bsz, c), F32), pltpu.VMEM((tm, c), F32),
                        pltpu.VMEM((tm, c), F32), pltpu.VMEM((bsz, c), F32)],
        compiler_params=_cparams(("arbitrary",)),
        name="a2_rglru",
    )(h0, h0, pre, w_in, conv_w, conv_b, wg, ga_b, gx_b, lam, w_out, post)


def _a3_body(h_ref, pre_ref, wgu_ref, wd_ref, post_ref, out_ref, tr_s, *, npad, bsz, tt, tf):
    i = pl.program_id(0)

    @pl.when(i < npad)
    def _():
        out_ref[...] = jnp.zeros_like(out_ref)

    @pl.when(i >= npad)
    def _():
        h = h_ref[...]
        xn = _rms(h, pre_ref[...]).astype(BF16)
        ff = wd_ref.shape[0]
        acc = None
        for c in range(ff // tf):
            g = _dot(xn, wgu_ref[:, c * tf:(c + 1) * tf])
            u = _dot(xn, wgu_ref[:, ff + c * tf:ff + (c + 1) * tf])
            part = _dot((g * _sigmoid(g) * u).astype(BF16), wd_ref[c * tf:(c + 1) * tf, :])
            acc = part if acc is None else acc + part
        res = h + _rms(acc, post_ref[...])
        for c in range(tr_s.shape[0]):
            tr_s[c] = res[:, c * LANES:(c + 1) * LANES]
        for b in range(bsz):
            for c in range(tr_s.shape[0]):
                out_ref[b, :, c * LANES:(c + 1) * LANES] = tr_s[c, pl.ds(b, tt, stride=bsz), :]


def _a3(h1, pre, w_gu, w_d, post, *, bsz, tt, npad, tf):
    r, d = h1.shape
    tm = tt * bsz
    nblk = r // tm
    fixed = lambda i: (0, 0)
    resident = lambda a: pl.BlockSpec(a.shape, fixed, pipeline_mode=pl.Buffered(1))
    return pl.pallas_call(
        functools.partial(_a3_body, npad=npad, bsz=bsz, tt=tt, tf=tf),
        grid=(npad + nblk,),
        in_specs=[pl.BlockSpec((tm, d), lambda i: (jnp.maximum(i - npad, 0), 0)), pl.BlockSpec((1, d), fixed),
                  resident(w_gu), resident(w_d), pl.BlockSpec((1, d), fixed)],
        out_specs=pl.BlockSpec((bsz, tt, d), lambda i: (0, i, 0)),
        out_shape=jax.ShapeDtypeStruct((bsz, (npad + nblk) * tt, d), F32),
        scratch_shapes=[pltpu.VMEM((d // LANES, tm, LANES), F32)],
        compiler_params=_cparams(("arbitrary",)),
        name="a3_dense_ffn",
    )(h1, pre, w_gu, w_d, post)


def _b1_body(h_ref, kvn_ref, bpre_ref, wc_ref, wka_ref, wkb_ref, latn_ref, wk_ref, wv_ref,
             wdq_ref, qn_ref, wq_ref, tq_ref, tkc_ref, tks_ref, q_ref, k_ref, v_ref, *, nh):
    h = h_ref[...]
    hn = h * lax.rsqrt(jnp.mean(h * h, axis=-1, keepdims=True) + RMS_EPS)
    ukv = (hn * kvn_ref[...]).astype(BF16)
    uq = (hn * bpre_ref[...]).astype(BF16)

    ckv = _rms(_dot(ukv, wc_ref[...]), latn_ref[...]).astype(BF16)
    kr = _dot(ukv, wka_ref[...]) * tkc_ref[...] + _dot(ukv, wkb_ref[...]) * tks_ref[...]
    kn = _dot(ckv, wk_ref[...])
    for hd in range(nh):
        sl = slice(hd * HEAD_PAD, (hd + 1) * HEAD_PAD)
        k_ref[:, sl] = (kn[:, sl] + kr).astype(BF16)
    v_ref[...] = _dot(ckv, wv_ref[...]).astype(BF16)

    cq = _rms(_dot(uq, wdq_ref[...]), qn_ref[...]).astype(BF16)
    qr = _dot(cq, wq_ref[...])
    tq = tq_ref[...]
    for hd in range(nh):
        sl = slice(hd * HEAD_PAD, (hd + 1) * HEAD_PAD)
        q_ref[:, sl] = (qr[:, sl] * tq).astype(BF16)


def _b1(h2, kvn, bpre, wc, wka, wkb, latn, wk, wv, wdq, qn, wq, tq, tkc, tks, *, tm, nh):
    r, d = h2.shape
    nper = tq.shape[0] // tm
    row = lambda i: (i, 0)
    tab = lambda i: (i % nper, 0)
    fixed = lambda i: (0, 0)
    full = lambda a: pl.BlockSpec(a.shape, fixed)
    return pl.pallas_call(
        functools.partial(_b1_body, nh=nh),
        grid=(r // tm,),
        in_specs=[pl.BlockSpec((tm, d), row), full(kvn), full(bpre), full(wc), full(wka), full(wkb),
                  full(latn), full(wk), full(wv), full(wdq), full(qn), full(wq),
                  pl.BlockSpec((tm, HEAD_PAD), tab), pl.BlockSpec((tm, HEAD_PAD), tab),
                  pl.BlockSpec((tm, HEAD_PAD), tab)],
        out_specs=[pl.BlockSpec((tm, nh * HEAD_PAD), row), pl.BlockSpec((tm, nh * HEAD_PAD), row),
                   pl.BlockSpec((tm, nh * V_HEAD), row)],
        out_shape=[jax.ShapeDtypeStruct((r, nh * HEAD_PAD), BF16),
                   jax.ShapeDtypeStruct((r, nh * HEAD_PAD), BF16),
                   jax.ShapeDtypeStruct((r, nh * V_HEAD), BF16)],
        compiler_params=_cparams(("arbitrary",)),
        name="b1_qkv_proj",
    )(h2, kvn, bpre, wc, wka, wkb, latn, wk, wv, wdq, qn, wq, tq, tkc, tks)


def _tile2(x):
    return jnp.concatenate([x, x], axis=1)


def _b2_body(q_ref, k_ref, v_ref, o_ref, vext_s, m0_s, m1_s, acc0_s, acc1_s, *, meta0, seq0, nq, qb, kb):
    pair = 2 * V_HEAD
    t_pad = v_ref.shape[1]
    m_s = (m0_s, m1_s)
    acc_s = (acc0_s, acc1_s)
    vext_s[meta0:, 0:pair] = v_ref[0, meta0:, :]
    vext_s[meta0:, pair:] = jnp.ones((t_pad - meta0, pair), BF16)
    o_ref[0, 0:seq0, :] = jnp.zeros((seq0, pair), o_ref.dtype)
    lane = lax.broadcasted_iota(I32, (qb, pair), 1)
    neg = jnp.finfo(F32).min

    def update(hh, qh, qlo, krows, masked):
        s = _dot_t(qh[qlo:], k_ref[0, krows, hh * HEAD_PAD:(hh + 1) * HEAD_PAD])
        if masked:
            rowi = lax.broadcasted_iota(I32, s.shape, 0)
            coli = lax.broadcasted_iota(I32, s.shape, 1)
            s = jnp.where(coli <= rowi, s, neg)
        m_old = m_s[hh][qlo:, :]
        m_new = jnp.maximum(m_old, jnp.max(s, axis=-1, keepdims=True))
        alpha = jnp.exp2(m_old - m_new)
        p = jnp.exp2(s - _tile2(m_new))
        acc_s[hh][qlo:, :] = _tile2(alpha) * acc_s[hh][qlo:, :] + _dot(p.astype(BF16), vext_s[krows, :])
        m_s[hh][qlo:, :] = m_new

    def qblock(i, _):
        q0 = pl.multiple_of(seq0 + i * qb, kb)
        qs = [q_ref[0, pl.ds(q0, qb), hh * HEAD_PAD:(hh + 1) * HEAD_PAD] for hh in range(2)]
        for hh in range(2):
            s = _dot_t(qs[hh], k_ref[0, meta0:seq0, hh * HEAD_PAD:(hh + 1) * HEAD_PAD])
            m = jnp.max(s, axis=-1, keepdims=True)
            p = jnp.exp2(s - m)
            acc_s[hh][...] = _dot(p.astype(BF16), vext_s[meta0:seq0, :])
            m_s[hh][...] = jnp.broadcast_to(m, (qb, pair))

        def kv_step(j, _):
            krows = pl.ds(pl.multiple_of(seq0 + j * kb, kb), kb)
            for hh in range(2):
                update(hh, qs[hh], 0, krows, False)
            return 0

        lax.fori_loop(0, i * (qb // kb), kv_step, 0)
        for jj in range(qb // kb):
            krows = pl.ds(pl.multiple_of(q0 + jj * kb, kb), kb)
            for hh in range(2):
                update(hh, qs[hh], jj * kb, krows, True)
        res = []
        for hh in range(2):
            acc = acc_s[hh][...]
            res.append(acc[:, :pair] / acc[:, pair:])
        o_ref[0, pl.ds(q0, qb), :] = jnp.where(lane < V_HEAD, res[0], res[1]).astype(o_ref.dtype)
        return 0

    lax.fori_loop(0, nq, qblock, 0)


def _b2(q, k, v, *, nh, meta0, seq0, seq, qb):
    bsz, t_pad, _ = q.shape
    npair = nh // 2
    qk = pl.BlockSpec((1, t_pad, 2 * HEAD_PAD), lambda b, p: (b, 0, p))
    vo = pl.BlockSpec((1, t_pad, 2 * V_HEAD), lambda b, p: (b, 0, p))
    return pl.pallas_call(
        functools.partial(_b2_body, meta0=meta0, seq0=seq0, nq=seq // qb, qb=qb, kb=ATT_BLK),
        grid=(bsz, npair),
        in_specs=[qk, qk, vo],
        out_specs=vo,
        out_shape=jax.ShapeDtypeStruct(v.shape, BF16),
        scratch_shapes=[pltpu.VMEM((t_pad, 4 * V_HEAD), BF16),
                        pltpu.VMEM((qb, 2 * V_HEAD), F32), pltpu.VMEM((qb, 2 * V_HEAD), F32),
                        pltpu.VMEM((qb, 4 * V_HEAD), F32), pltpu.VMEM((qb, 4 * V_HEAD), F32)],
        compiler_params=_cparams(("arbitrary", "arbitrary")),
        name="b2_attention",
    )(q, k, v)


def _b3_body(o_ref, h_ref, wo_ref, post_ref, mpre_ref, rw_ref, rb_ref, h3_ref, xn_ref, route_ref, cnt_ref,
             cnt_s):
    @pl.when(pl.program_id(0) == 0)
    def _():
        cnt_s[...] = jnp.zeros_like(cnt_s)

    mix = _dot(o_ref[0], wo_ref[...])
    h3 = h_ref[0] + _rms(mix, post_ref[...])
    h3_ref[...] = h3
    xn = _rms(h3, mpre_ref[...])
    xn_ref[...] = xn
    logits = _dot(xn.astype(BF16), rw_ref[...]) + rb_ref[...]
    lane = lax.broadcasted_iota(I32, logits.shape, 1).astype(F32)
    big = float(LANES)
    m1 = jnp.max(logits, axis=-1, keepdims=True)
    i1 = jnp.min(jnp.where(logits == m1, lane, big), axis=-1, keepdims=True)
    rest = jnp.where(lane == i1, -jnp.inf, logits)
    m2 = jnp.max(rest, axis=-1, keepdims=True)
    i2 = jnp.min(jnp.where(rest == m2, lane, big), axis=-1, keepdims=True)
    e2 = jnp.exp(m2 - m1)
    g1 = 1.0 / (1.0 + e2)
    g2 = e2 / (1.0 + e2)
    route_ref[...] = jnp.where(lane == 0.0, g1, jnp.where(lane == 1.0, g2,
                               jnp.where(lane == 2.0, i1, jnp.where(lane == 3.0, i2, 0.0))))
    onehot = jnp.where((lane == i1) | (lane == i2), 1.0, 0.0)
    cnt_s[...] += jnp.sum(onehot, axis=0, keepdims=True)
    cnt_ref[...] = jnp.broadcast_to(cnt_s[...], cnt_ref.shape)


def _b3(o, h2, w_o, post, mpre, rw, rb, *, tm, blk0, nper):
    bsz, _, d = h2.shape
    src = lambda i: (i // nper, blk0 + i % nper, 0)
    dst = lambda i: (i, 0)
    fixed = lambda i: (0, 0)
    nblk = bsz * nper
    rows = nblk * tm
    return pl.pallas_call(
        _b3_body,
        grid=(nblk,),
        in_specs=[pl.BlockSpec((1, tm, o.shape[2]), src), pl.BlockSpec((1, tm, d), src),
                  pl.BlockSpec(w_o.shape, fixed), pl.BlockSpec((1, d), fixed), pl.BlockSpec((1, d), fixed),
                  pl.BlockSpec(rw.shape, fixed), pl.BlockSpec((1, LANES), fixed)],
        out_specs=[pl.BlockSpec((tm, d), dst), pl.BlockSpec((tm, d), dst), pl.BlockSpec((tm, LANES), dst),
                   pl.BlockSpec((8, LANES), fixed)],
        out_shape=[jax.ShapeDtypeStruct((rows, d), F32), jax.ShapeDtypeStruct((rows, d), F32),
                   jax.ShapeDtypeStruct((rows, LANES), F32), jax.ShapeDtypeStruct((8, LANES), F32)],
        scratch_shapes=[pltpu.VMEM((1, LANES), F32)],
        compiler_params=_cparams(("arbitrary",)),
        name="b3_oproj_router",
    )(o, h2, w_o, post, mpre, rw, rb)


def _b4_body(route_ref, cnt_ref, meta_ref, run_s, off_s, *, tile):
    route = route_ref[...]
    tm = route.shape[0]
    lane = lax.broadcasted_iota(I32, route.shape, 1).astype(F32)

    @pl.when(pl.program_id(0) == 0)
    def _():
        total = cnt_ref[0:1, :]
        ntiles = jnp.floor((total + (tile - 1)) * (1.0 / tile))
        rr = lax.broadcasted_iota(I32, (LANES, LANES), 0)
        cc = lax.broadcasted_iota(I32, (LANES, LANES), 1)
        upper = jnp.where(rr < cc, 1.0, 0.0).astype(BF16)
        before = _dot(jnp.broadcast_to(ntiles, (8, LANES)).astype(BF16), upper)
        off_s[...] = before[0:1, :] * float(tile)
        run_s[...] = jnp.zeros_like(run_s)

    i1 = jnp.sum(jnp.where(lane == 2.0, route, 0.0), axis=-1, keepdims=True)
    i2 = jnp.sum(jnp.where(lane == 3.0, route, 0.0), axis=-1, keepdims=True)
    oh1 = lane == i1
    oh2 = lane == i2
    onehot = jnp.where(oh1 | oh2, 1.0, 0.0)
    rr = lax.broadcasted_iota(I32, (tm, tm), 0)
    cc = lax.broadcasted_iota(I32, (tm, tm), 1)
    lower = jnp.where(cc < rr, 1.0, 0.0).astype(BF16)
    run = run_s[...]
    pos = _dot(lower, onehot.astype(BF16)) + (run + off_s[...])
    p1 = jnp.sum(jnp.where(oh1, pos, 0.0), axis=-1, keepdims=True)
    p2 = jnp.sum(jnp.where(oh2, pos, 0.0), axis=-1, keepdims=True)
    packed = jnp.where(lane == 0.0, p1, jnp.where(lane == 1.0, p2, 0.0))
    meta_ref[0] = jnp.transpose(packed)[0:8, :].astype(I32)
    run_s[...] = run + jnp.sum(onehot, axis=0, keepdims=True)


def _b4(route, cnt, *, tm, tile):
    nblk = route.shape[0] // tm
    return pl.pallas_call(
        functools.partial(_b4_body, tile=tile),
        grid=(nblk,),
        in_specs=[pl.BlockSpec((tm, LANES), lambda i: (i, 0)), pl.BlockSpec((8, LANES), lambda i: (0, 0))],
        out_specs=pl.BlockSpec((1, 8, tm), lambda i: (i, 0, 0)),
        out_shape=jax.ShapeDtypeStruct((nblk, 8, tm), I32),
        scratch_shapes=[pltpu.VMEM((1, LANES), F32), pltpu.VMEM((1, LANES), F32)],
        compiler_params=_cparams(("arbitrary",)),
        name="b4_route_rank",
    )(route, cnt)


def _row(ref, pos):
    return ref.at[pos >> 3, pl.ds(pos & (SUBLANES - 1), 1)]


def _b5_body(meta_ref, xn_ref, xg_in_ref, xg_ref, sem, *, tm):
    del xg_in_ref

    def issue(g, _):
        for u in range(SUBLANES):
            for k in range(TOP_K):
                pos = meta_ref[k * tm + g * SUBLANES + u]
                pltpu.make_async_copy(xn_ref.at[g, pl.ds(u, 1)], _row(xg_ref, pos), sem).start()
        return 0

    lax.fori_loop(0, tm // SUBLANES, issue, 0)
    for k in range(TOP_K):
        pltpu.make_async_copy(xn_ref, xg_ref.at[pl.ds(0, tm // SUBLANES)], sem).wait()


def _b5(meta, xn, xg0, *, tm):
    nblk = meta.shape[0] // (TOP_K * tm)
    return pl.pallas_call(
        functools.partial(_b5_body, tm=tm),
        grid=(nblk,),
        in_specs=[pl.BlockSpec((TOP_K * tm,), lambda i: (i,), memory_space=pltpu.SMEM),
                  pl.BlockSpec((tm // SUBLANES,) + xn.shape[1:], lambda i: (i, 0, 0)),
                  pl.BlockSpec(memory_space=pl.ANY)],
        out_specs=pl.BlockSpec(memory_space=pl.ANY),
        scratch_shapes=[pltpu.SemaphoreType.DMA(())],
        out_shape=jax.ShapeDtypeStruct(xg0.shape, xg0.dtype),
        input_output_aliases={2: 0},
        compiler_params=_cparams(("arbitrary",)),
        name="b5_dispatch",
    )(meta, xn, xg0)


def _b6_body(te_ref, tv_ref, x_ref, wg_ref, wu_ref, wd_ref, y_ref, xb_s, acc_s):
    j = pl.program_id(0)
    f = pl.program_id(1)

    @pl.when(f == 0)
    def _():
        xb_s[...] = x_ref[...].astype(BF16)
        acc_s[...] = jnp.zeros_like(acc_s)

    @pl.when(tv_ref[j] > 0)
    def _():
        xb = xb_s[...]
        g = _dot(xb, wg_ref[0])
        u = _dot(xb, wu_ref[0])
        acc_s[...] += _dot((g * _sigmoid(g) * u).astype(BF16), wd_ref[0])

    @pl.when(f == pl.num_programs(1) - 1)
    def _():
        y_ref[...] = acc_s[...]


def _b6(te, tv, xg, w_gu, w_d, *, tm, tf):
    rg, d = xg.shape
    eff = w_d.shape[1]
    nf = eff // tf
    nt = rg // tm
    return pl.pallas_call(
        _b6_body,
        grid_spec=pltpu.PrefetchScalarGridSpec(
            num_scalar_prefetch=2,
            grid=(nt, nf),
            in_specs=[pl.BlockSpec((tm, d), lambda j, f, te, tv: (j, 0)),
                      pl.BlockSpec((1, d, tf), lambda j, f, te, tv: (te[j], 0, f)),
                      pl.BlockSpec((1, d, tf), lambda j, f, te, tv: (te[j], 0, nf + f)),
                      pl.BlockSpec((1, tf, d), lambda j, f, te, tv: (te[j], f, 0))],
            out_specs=pl.BlockSpec((tm, d), lambda j, f, te, tv: (j, 0)),
            scratch_shapes=[pltpu.VMEM((tm, d), BF16), pltpu.VMEM((tm, d), F32)]),
        out_shape=jax.ShapeDtypeStruct((rg, d), F32),
        compiler_params=_cparams(("arbitrary", "arbitrary")),
        name="b6_moe_ffn",
    )(te, tv, xg, w_gu, w_gu, w_d)


def _b7_body(meta_ref, meta_next_ref, route_ref, h_ref, post_ref, yg_ref, out_ref, buf_s, sem, *, tm):
    i = pl.program_id(0)
    slot = i % 2

    def gather_tile(mref, s):
        def issue(g, _):
            for u in range(SUBLANES):
                for k in range(TOP_K):
                    pos = mref[k * tm + g * SUBLANES + u]
                    pltpu.make_async_copy(_row(yg_ref, pos), buf_s.at[s, k, g, pl.ds(u, 1)], sem.at[s]).start()
            return 0

        lax.fori_loop(0, tm // SUBLANES, issue, 0)

    @pl.when(i == 0)
    def _():
        gather_tile(meta_ref, 0)

    @pl.when(i + 1 < pl.num_programs(0))
    def _():
        gather_tile(meta_next_ref, 1 - slot)

    for k in range(TOP_K):
        pltpu.make_async_copy(yg_ref.at[pl.ds(0, tm // SUBLANES)], buf_s.at[slot, k], sem.at[slot]).wait()
    route = route_ref[...]
    d = out_ref.shape[1]
    y = route[:, 0:1] * buf_s[slot, 0].reshape(tm, d) + route[:, 1:2] * buf_s[slot, 1].reshape(tm, d)
    out_ref[...] = h_ref[...] + _rms(y, post_ref[...])


def _b7(meta, route, h3, post, yg, *, tm):
    nblk = meta.shape[0] // (TOP_K * tm)
    d = h3.shape[1]
    row = lambda i: (i, 0)
    return pl.pallas_call(
        functools.partial(_b7_body, tm=tm),
        grid=(nblk,),
        in_specs=[pl.BlockSpec((TOP_K * tm,), lambda i: (i,), memory_space=pltpu.SMEM),
                  pl.BlockSpec((TOP_K * tm,), lambda i: (jnp.minimum(i + 1, nblk - 1),),
                               memory_space=pltpu.SMEM),
                  pl.BlockSpec((tm, LANES), row), pl.BlockSpec((tm, d), row),
                  pl.BlockSpec((1, d), lambda i: (0, 0)),
                  pl.BlockSpec(memory_space=pl.ANY)],
        out_specs=pl.BlockSpec((tm, d), row),
        scratch_shapes=[pltpu.VMEM((2, TOP_K, tm // SUBLANES, SUBLANES, d), F32),
                        pltpu.SemaphoreType.DMA((2,))],
        out_shape=jax.ShapeDtypeStruct((nblk * tm, d), F32),
        compiler_params=_cparams(("arbitrary",)),
        name="b7_combine",
    )(meta, meta, route, h3, post, yg)


def _pick(n, target, mult):
    best = mult
    for cand in range(mult, min(n, target) + 1, mult):
        if n % cand == 0:
            best = cand
    return best


def kernel(x, meta_tokens, a_pre_norm, a_w_in, a_conv_w, a_conv_b, a_gate_a_w, a_gate_a_b, a_gate_x_w, a_gate_x_b, a_lambda, a_w_out, a_post_norm, kv_norm, kv_w_dkv, kv_latent_norm, kv_w_ukv, b_pre_norm, b_w_dq, b_q_norm, b_w_uq, b_w_o, b_post_norm, f_pre_norm, f_w_gate_up, f_w_down, f_post_norm, m_pre_norm, m_router_w, m_router_b, m_w_gate_up, m_w_down, m_post_norm):
    bsz, seq, d = x.shape
    n_meta = meta_tokens.shape[0]
    assert a_pre_norm.shape[0] == 1 and b_pre_norm.shape[0] == 1, "one RG-LRU layer + one MLA layer"
    t_all = n_meta + seq
    assert t_all % TT == 0 and seq % ATT_BLK == 0 and bsz % 8 == 0 and n_meta % 16 == 0
    rows = t_all * bsz
    row = lambda v: v.reshape(1, -1).astype(F32)
    bf = lambda w: w.astype(BF16)

    h0 = jnp.concatenate([jnp.broadcast_to(meta_tokens.astype(F32)[:, None, :], (n_meta, bsz, d)),
                          jnp.transpose(x, (1, 0, 2))], axis=0).reshape(rows, d)

    wg = bf(jnp.concatenate([a_gate_a_w[0], a_gate_x_w[0]], axis=-1))
    h1 = _a2(h0, row(a_pre_norm[0]), bf(a_w_in[0]), a_conv_w[0].astype(F32), row(a_conv_b[0]), wg,
             row(a_gate_a_b[0]), row(a_gate_x_b[0]), row(a_lambda[0]), bf(a_w_out[0]), row(a_post_norm[0]),
             tt=TT_SMALL, bsz=bsz)
    d_ff = f_w_down.shape[1]
    h2 = _a3(h1, row(f_pre_norm[0]), bf(f_w_gate_up[0]), bf(f_w_down[0]), row(f_post_norm[0]),
             bsz=bsz, tt=TT, npad=PAD_T // TT, tf=_pick(d_ff, 1024, LANES))
    t_pad = PAD_T + t_all
    seq0 = PAD_T + n_meta
    assert seq0 % ATT_BLK == 0 and t_pad % (TT_SMALL * bsz) == 0

    kv_lora = kv_latent_norm.shape[0]
    nh = b_w_uq.shape[2] // (QK_NOPE + QK_ROPE)
    hr = QK_ROPE // 2
    scale = (QK_NOPE + QK_ROPE) ** -0.5 * math.log2(math.e)
    inv = ROPE_THETA ** (-jnp.arange(0, QK_ROPE, 2, dtype=F32) / QK_ROPE)
    pos = jnp.maximum(jnp.arange(t_pad, dtype=F32) - PAD_T, 0.0)
    ang = pos[:, None] * inv[None, :]
    cos, sin = jnp.cos(ang), jnp.sin(ang)
    one = jnp.ones((t_pad, QK_NOPE), F32)
    zero = jnp.zeros((t_pad, QK_NOPE), F32)
    tq = scale * jnp.concatenate([one, cos, cos, -sin, sin], axis=1)
    tkc = jnp.concatenate([zero, cos, cos, cos, cos], axis=1)
    tks = jnp.concatenate([zero, -sin, sin, -sin, sin], axis=1)

    w_c = kv_w_dkv[:, :kv_lora]
    k1 = kv_w_dkv[:, kv_lora:kv_lora + hr]
    k2 = kv_w_dkv[:, kv_lora + hr:]
    zk = jnp.zeros((d, QK_NOPE), F32)
    wka = jnp.concatenate([zk, k1, k2, k1, k2], axis=1)
    wkb = jnp.concatenate([zk, k2, k1, k2, k1], axis=1)
    ukv = kv_w_ukv.reshape(kv_lora, nh, QK_NOPE + V_HEAD)
    wk = jnp.concatenate([ukv[:, :, :QK_NOPE], jnp.zeros((kv_lora, nh, HEAD_PAD - QK_NOPE), F32)],
                         axis=2).reshape(kv_lora, nh * HEAD_PAD)
    wv = ukv[:, :, QK_NOPE:].reshape(kv_lora, nh * V_HEAD)
    uq = b_w_uq[0].reshape(-1, nh, QK_NOPE + QK_ROPE)
    qn_, q1, q2 = uq[:, :, :QK_NOPE], uq[:, :, QK_NOPE:QK_NOPE + hr], uq[:, :, QK_NOPE + hr:]
    wq = jnp.concatenate([qn_, q1, q2, q2, q1], axis=2).reshape(-1, nh * HEAD_PAD)
    q, k, v = _b1(h2.reshape(bsz * t_pad, d), row(kv_norm), row(b_pre_norm[0]), bf(w_c), bf(wka), bf(wkb),
                  row(kv_latent_norm), bf(wk), bf(wv), bf(b_w_dq[0]), row(b_q_norm[0]), bf(wq), tq, tkc, tks,
                  tm=TT_SMALL * bsz, nh=nh)

    o = _b2(q.reshape(bsz, t_pad, nh * HEAD_PAD), k.reshape(bsz, t_pad, nh * HEAD_PAD),
            v.reshape(bsz, t_pad, nh * V_HEAD), nh=nh, meta0=PAD_T, seq0=seq0, seq=seq,
            qb=_pick(seq, ATT_QBLK, ATT_BLK))

    rows_seq = seq * bsz
    tm1 = ATT_BLK
    assert rows_seq % MOE_TILE == 0
    ne = m_router_w.shape[2]
    rw = bf(jnp.concatenate([m_router_w[0], jnp.zeros((d, LANES - ne), F32)], axis=1))
    rb = jnp.concatenate([m_router_b[0].astype(F32), jnp.full((LANES - ne,), -1e30, F32)]).reshape(1, LANES)
    h3, xn, route, cnt = _b3(o, h2, bf(b_w_o[0]), row(b_post_norm[0]), row(m_pre_norm[0]), rw, rb,
                             tm=tm1, blk0=seq0 // tm1, nper=seq // tm1)

    tmg = MOE_TILE
    meta = _b4(route, cnt, tm=tm1, tile=tmg)[:, :TOP_K, :].reshape(-1)
    counts = cnt[0, :ne].astype(I32)
    padded = ((counts + tmg - 1) // tmg) * tmg
    ends = jnp.cumsum(padded)
    nt = (TOP_K * rows_seq) // tmg + ne
    tile_start = jnp.arange(nt, dtype=I32) * tmg
    tv = (tile_start < ends[-1]).astype(I32)
    last = jnp.maximum(ends[-1] // tmg - 1, 0)
    te_raw = jnp.sum((tile_start[:, None] >= ends[None, :]).astype(I32), axis=1)
    te = jnp.minimum(jnp.where(tv > 0, te_raw, te_raw[last]), ne - 1).astype(I32)

    grp = lambda a: a.reshape(a.shape[0] // SUBLANES, SUBLANES, d)
    xg = _b5(meta, grp(xn), jnp.zeros((nt * tmg // SUBLANES, SUBLANES, d), F32), tm=tm1)
    e_ff = m_w_down.shape[2]
    yg = _b6(te, tv, xg.reshape(nt * tmg, d), bf(m_w_gate_up[0]), bf(m_w_down[0]), tm=tmg,
             tf=_pick(e_ff, MOE_FF_CHUNK, LANES))
    h4 = _b7(meta, route, h3, row(m_post_norm[0]), grp(yg), tm=tm1)
    return h4.reshape(bsz, seq, d)
```

```python
import functools
import math

import jax
import jax.numpy as jnp
from jax import lax
from jax.experimental import pallas as pl
from jax.experimental.pallas import tpu as pltpu

F32 = jnp.float32
BF16 = jnp.bfloat16
I32 = jnp.int32

RMS_EPS = 1e-6
RG_C = 8.0
ROPE_THETA = 10000.0
QK_NOPE = 64
QK_ROPE = 32
V_HEAD = 64
HEAD_PAD = 128
TOP_K = 2
LANES = 128
SUBLANES = 8

TT = 48
TT_SMALL = 24
PAD_T = 240
ATT_BLK = 256
ATT_KBLK = 256
ATT_QBLK = 1024
ATT_HEADS = 2
MOE_TILE = 512
MOE_FF_CHUNK = 1792
VMEM_LIMIT = 56 * 1024 * 1024


def _cparams(sem):
    return pltpu.CompilerParams(dimension_semantics=sem, vmem_limit_bytes=VMEM_LIMIT)


def _rms(x, g):
    ms = jnp.mean(x * x, axis=-1, keepdims=True)
    return x * lax.rsqrt(ms + RMS_EPS) * g


def _sigmoid(x):
    return 0.5 * jnp.tanh(0.5 * x) + 0.5


def _dot(a, b):
    return jnp.dot(a, b, preferred_element_type=F32)


def _dot_t(a, b):
    return lax.dot_general(a, b, (((1,), (1,)), ((), ())), preferred_element_type=F32)


def _a2_body(h_ref, hnext_ref, pre_ref, win_ref, cw_ref, cb_ref, wg_ref, gab_ref, gxb_ref, lam_ref,
             wo_ref, pn_ref, out_ref, yx_s, ext_s, a_s, b_s, carry_s, *, nb, tt, bsz, cw):
    tm = tt * bsz
    tail = (cw - 1) * bsz
    c = ext_s.shape[1]
    i = pl.program_id(0)
    slot = i % 2

    def project(href, s):
        xn = _rms(href[...], pre_ref[...]).astype(BF16)
        yx_s[s, :, 0:c] = _dot(xn, win_ref[:, 0:c]).astype(BF16)
        yx_s[s, :, c:] = _dot(xn, win_ref[:, c:]).astype(BF16)

    @pl.when(i == 0)
    def _():
        ext_s[0:tail, :] = jnp.zeros((tail, c), F32)
        carry_s[...] = jnp.zeros_like(carry_s)
        project(h_ref, 0)

    ext_s[tail:tail + tm, :] = yx_s[slot, :, c:].astype(F32)

    xn_next = _rms(hnext_ref[...], pre_ref[...]).astype(BF16)
    pw = 2 * LANES
    pieces = [(s0, min(s0 + pw, 2 * c)) for s0 in range(0, 2 * c, pw)]

    def project_piece(p):
        lo, hi = pieces[p]
        yx_s[1 - slot, :, lo:hi] = _dot(xn_next, win_ref[:, lo:hi]).astype(BF16)

    xc = cb_ref[...] + ext_s[0:tm, :] * cw_ref[0:1, :]
    for k in range(1, cw):
        xc = xc + ext_s[k * bsz:k * bsz + tm, :] * cw_ref[k:k + 1, :]
    ext_s[0:tail, :] = ext_s[tm:tm + tail, :]

    neg_lam = -lam_ref[...]
    sp = jnp.maximum(neg_lam, 0.0) + jnp.log1p(jnp.exp(-jnp.abs(neg_lam)))
    rate = sp * (-RG_C * math.log2(math.e))
    xcb = xc.astype(BF16)
    per_block = -(-len(pieces) // nb)
    for n in range(nb):
        for p in range(n * per_block, min((n + 1) * per_block, len(pieces))):
            project_piece(p)
        sl = slice(n * LANES, (n + 1) * LANES)
        g = _dot(xcb[:, sl], wg_ref[n])
        r = _sigmoid(g[:, :LANES] + gab_ref[:, sl])
        ig = _sigmoid(g[:, LANES:] + gxb_ref[:, sl])
        a = jnp.exp2(r * rate[:, sl])
        a_s[:, sl] = a
        u = 1.0 - a * a
        root = jnp.where(u > 0.0, u * lax.rsqrt(u), 0.0)
        b_s[:, sl] = root * (ig * xc[:, sl])

    def step(t, h):
        rows = pl.ds(pl.multiple_of(t * bsz, bsz), bsz)
        h = a_s[rows, :] * h + b_s[rows, :]
        b_s[rows, :] = h
        return h

    carry_s[...] = lax.fori_loop(0, tt, step, carry_s[...], unroll=4)

    z = (jax.nn.gelu(yx_s[slot, :, 0:c].astype(F32)) * b_s[...]).astype(BF16)
    mix = _dot(z, wo_ref[...])
    out_ref[...] = h_ref[...] + _rms(mix, pn_ref[...])


def _a2(h0, pre, w_in, conv_w, conv_b, wg, ga_b, gx_b, lam, w_out, post, *, tt, bsz):
    r, d = h0.shape
    c = w_in.shape[1] // 2
    nb = wg.shape[0]
    cw = conv_w.shape[0]
    tm = tt * bsz
    nblk = r // tm
    row = lambda i: (i, 0)
    fixed2 = lambda i: (0, 0)
    fixed3 = lambda i: (0, 0, 0)
    return pl.pallas_call(
        functools.partial(_a2_body, nb=nb, tt=tt, bsz=bsz, cw=cw),
        grid=(nblk,),
        in_specs=[pl.BlockSpec((tm, d), row),
                  pl.BlockSpec((tm, d), lambda i: (jnp.minimum(i + 1, nblk - 1), 0)),
                  pl.BlockSpec((1, d), fixed2),
                  pl.BlockSpec((d, 2 * c), fixed2, pipeline_mode=pl.Buffered(1)),
                  pl.BlockSpec((cw, c), fixed2), pl.BlockSpec((1, c), fixed2),
                  pl.BlockSpec((nb, LANES, 2 * LANES), fixed3),
                  pl.BlockSpec((1, c), fixed2), pl.BlockSpec((1, c), fixed2), pl.BlockSpec((1, c), fixed2),
                  pl.BlockSpec((c, d), fixed2, pipeline_mode=pl.Buffered(1)), pl.BlockSpec((1, d), fixed2)],
        out_specs=pl.BlockSpec((tm, d), row),
        out_shape=jax.ShapeDtypeStruct((r, d), F32),
        scratch_shapes=[pltpu.VMEM((2, tm, 2 * c), BF16),
                        pltpu.VMEM((tm + (cw - 1) * bsz, c), F32), pltpu.VMEM((tm, c), F32),
                        pltpu.VMEM((tm, c), F32), pltpu.VMEM((bsz, c), F32)],
        compiler_params=_cparams(("arbitrary",)),
        name="a2_rglru",
    )(h0, h0, pre, w_in, conv_w, conv_b, wg, ga_b, gx_b, lam, w_out, post)


def _a3_body(h_ref, pre_ref, wgu_ref, wd_ref, post_ref, out_ref, tr_s, *, npad, bsz, tt, tf):
    i = pl.program_id(0)

    @pl.when(i < npad)
    def _():
        out_ref[...] = jnp.zeros_like(out_ref)

    @pl.when(i >= npad)
    def _():
        h = h_ref[...]
        xn = _rms(h, pre_ref[...]).astype(BF16)
        ff = wd_ref.shape[0]
        acc = None
        for c in range(ff // tf):
            g = _dot(xn, wgu_ref[:, c * tf:(c + 1) * tf])
            u = _dot(xn, wgu_ref[:, ff + c * tf:ff + (c + 1) * tf])
            part = _dot((g * _sigmoid(g) * u).astype(BF16), wd_ref[c * tf:(c + 1) * tf, :])
            acc = part if acc is None else acc + part
        res = h + _rms(acc, post_ref[...])
        for c in range(tr_s.shape[0]):
            tr_s[c] = res[:, c * LANES:(c + 1) * LANES]
        for b in range(bsz):
            for c in range(tr_s.shape[0]):
                out_ref[b, :, c * LANES:(c + 1) * LANES] = tr_s[c, pl.ds(b, tt, stride=bsz), :]


def _a3(h1, pre, w_gu, w_d, post, *, bsz, tt, npad, tf):
    r, d = h1.shape
    tm = tt * bsz
    nblk = r // tm
    fixed = lambda i: (0, 0)
    resident = lambda a: pl.BlockSpec(a.shape, fixed, pipeline_mode=pl.Buffered(1))
    return pl.pallas_call(
        functools.partial(_a3_body, npad=npad, bsz=bsz, tt=tt, tf=tf),
        grid=(npad + nblk,),
        in_specs=[pl.BlockSpec((tm, d), lambda i: (jnp.maximum(i - npad, 0), 0)), pl.BlockSpec((1, d), fixed),
                  resident(w_gu), resident(w_d), pl.BlockSpec((1, d), fixed)],
        out_specs=pl.BlockSpec((bsz, tt, d), lambda i: (0, i, 0)),
        out_shape=jax.ShapeDtypeStruct((bsz, (npad + nblk) * tt, d), F32),
        scratch_shapes=[pltpu.VMEM((d // LANES, tm, LANES), F32)],
        compiler_params=_cparams(("arbitrary",)),
        name="a3_dense_ffn",
    )(h1, pre, w_gu, w_d, post)


def _b1_body(h_ref, kvn_ref, bpre_ref, wc_ref, wka_ref, wkb_ref, latn_ref, wk_ref, wv_ref,
             wdq_ref, qn_ref, wq_ref, tq_ref, tkc_ref, tks_ref, q_ref, k_ref, v_ref, *, nh):
    h = h_ref[...]
    hn = h * lax.rsqrt(jnp.mean(h * h, axis=-1, keepdims=True) + RMS_EPS)
    ukv = (hn * kvn_ref[...]).astype(BF16)
    uq = (hn * bpre_ref[...]).astype(BF16)

    ckv = _rms(_dot(ukv, wc_ref[...]), latn_ref[...]).astype(BF16)
    kr = _dot(ukv, wka_ref[...]) * tkc_ref[...] + _dot(ukv, wkb_ref[...]) * tks_ref[...]
    kn = _dot(ckv, wk_ref[...])
    for hd in range(nh):
        sl = slice(hd * HEAD_PAD, (hd + 1) * HEAD_PAD)
        k_ref[:, sl] = (kn[:, sl] + kr).astype(BF16)
    v_ref[...] = _dot(ckv, wv_ref[...]).astype(BF16)

    cq = _rms(_dot(uq, wdq_ref[...]), qn_ref[...]).astype(BF16)
    qr = _dot(cq, wq_ref[...])
    tq = tq_ref[...]
    for hd in range(nh):
        sl = slice(hd * HEAD_PAD, (hd + 1) * HEAD_PAD)
        q_ref[:, sl] = (qr[:, sl] * tq).astype(BF16)


def _b1(h2, kvn, bpre, wc, wka, wkb, latn, wk, wv, wdq, qn, wq, tq, tkc, tks, *, tm, nh):
    r, d = h2.shape
    nper = tq.shape[0] // tm
    row = lambda i: (i, 0)
    tab = lambda i: (i % nper, 0)
    fixed = lambda i: (0, 0)
    full = lambda a: pl.BlockSpec(a.shape, fixed)
    return pl.pallas_call(
        functools.partial(_b1_body, nh=nh),
        grid=(r // tm,),
        in_specs=[pl.BlockSpec((tm, d), row), full(kvn), full(bpre), full(wc), full(wka), full(wkb),
                  full(latn), full(wk), full(wv), full(wdq), full(qn), full(wq),
                  pl.BlockSpec((tm, HEAD_PAD), tab), pl.BlockSpec((tm, HEAD_PAD), tab),
                  pl.BlockSpec((tm, HEAD_PAD), tab)],
        out_specs=[pl.BlockSpec((tm, nh * HEAD_PAD), row), pl.BlockSpec((tm, nh * HEAD_PAD), row),
                   pl.BlockSpec((tm, nh * V_HEAD), row)],
        out_shape=[jax.ShapeDtypeStruct((r, nh * HEAD_PAD), BF16),
                   jax.ShapeDtypeStruct((r, nh * HEAD_PAD), BF16),
                   jax.ShapeDtypeStruct((r, nh * V_HEAD), BF16)],
        compiler_params=_cparams(("arbitrary",)),
        name="b1_qkv_proj",
    )(h2, kvn, bpre, wc, wka, wkb, latn, wk, wv, wdq, qn, wq, tq, tkc, tks)


def _tile2(x):
    return jnp.concatenate([x, x], axis=1)


def _b2_body(q_ref, k_ref, v_ref, o_ref, *scratch, meta0, seq0, nq, qb, kb, nhs):
    pair = 2 * V_HEAD
    t_pad = v_ref.shape[1]
    nsub = qb // kb
    npairs = nhs // 2
    vext_s = scratch[0:npairs]
    m_s = scratch[npairs:npairs + nhs]
    acc_s = scratch[npairs + nhs:npairs + 2 * nhs]
    for pp in range(npairs):
        vext_s[pp][meta0:, 0:pair] = v_ref[0, meta0:, pp * pair:(pp + 1) * pair]
        vext_s[pp][meta0:, pair:] = jnp.ones((t_pad - meta0, pair), BF16)
    o_ref[0, 0:seq0, :] = jnp.zeros((seq0, npairs * pair), o_ref.dtype)
    lane = lax.broadcasted_iota(I32, (qb, pair), 1)
    neg = jnp.finfo(F32).min
    heads = range(nhs)

    def scores(hh, qh, qlo, krows):
        return _dot_t(qh[qlo:], k_ref[0, krows, hh * HEAD_PAD:(hh + 1) * HEAD_PAD])

    def absorb(hh, s, qlo, krows, masked):
        if masked:
            rowi = lax.broadcasted_iota(I32, s.shape, 0)
            coli = lax.broadcasted_iota(I32, s.shape, 1)
            s = jnp.where(coli <= rowi, s, neg)
        m_old = m_s[hh][qlo:, :]
        m_new = jnp.maximum(m_old, jnp.max(s, axis=-1, keepdims=True))
        alpha = jnp.exp2(m_old - m_new)
        p = jnp.exp2(s - jnp.concatenate([m_new] * (s.shape[1] // pair), axis=1))
        acc_s[hh][qlo:, :] = (_tile2(alpha) * acc_s[hh][qlo:, :]
                              + _dot(p.astype(BF16), vext_s[hh // 2][krows, :]))
        m_s[hh][qlo:, :] = m_new

    def qblock(i, _):
        q0 = pl.multiple_of(seq0 + i * qb, ATT_BLK)
        qs = [q_ref[0, pl.ds(q0, qb), hh * HEAD_PAD:(hh + 1) * HEAD_PAD] for hh in heads]
        for hh in heads:
            s = _dot_t(qs[hh], k_ref[0, meta0:seq0, hh * HEAD_PAD:(hh + 1) * HEAD_PAD])
            m = jnp.max(s, axis=-1, keepdims=True)
            p = jnp.exp2(s - m)
            acc_s[hh][...] = _dot(p.astype(BF16), vext_s[hh // 2][meta0:seq0, :])
            m_s[hh][...] = jnp.broadcast_to(m, (qb, pair))

        def kv_step(j, _):
            krows = pl.ds(pl.multiple_of(seq0 + j * kb, ATT_BLK), kb)
            for hh in heads:
                absorb(hh, scores(hh, qs[hh], 0, krows), 0, krows, False)
            return 0

        lax.fori_loop(0, i * nsub, kv_step, 0)
        for jj in range(nsub):
            krows = pl.ds(pl.multiple_of(q0 + jj * kb, ATT_BLK), kb)
            for hh in heads:
                absorb(hh, scores(hh, qs[hh], jj * kb, krows), jj * kb, krows, True)
        for pp in range(npairs):
            res = []
            for hh in (2 * pp, 2 * pp + 1):
                acc = acc_s[hh][...]
                res.append(acc[:, :pair] / acc[:, pair:])
            o_ref[0, pl.ds(q0, qb), pp * pair:(pp + 1) * pair] = jnp.where(
                lane < V_HEAD, res[0], res[1]).astype(o_ref.dtype)
        return 0

    lax.fori_loop(0, nq, qblock, 0)


def _b2(q, k, v, *, nh, meta0, seq0, seq, qb, nhs):
    bsz, t_pad, _ = q.shape
    qk = pl.BlockSpec((1, t_pad, nhs * HEAD_PAD), lambda b, p: (b, 0, p))
    vo = pl.BlockSpec((1, t_pad, nhs * V_HEAD), lambda b, p: (b, 0, p))
    return pl.pallas_call(
        functools.partial(_b2_body, meta0=meta0, seq0=seq0, nq=seq // qb, qb=qb, kb=min(ATT_KBLK, qb), nhs=nhs),
        grid=(bsz, nh // nhs),
        in_specs=[qk, qk, vo],
        out_specs=vo,
        out_shape=jax.ShapeDtypeStruct(v.shape, BF16),
        scratch_shapes=([pltpu.VMEM((t_pad, 4 * V_HEAD), BF16)] * (nhs // 2)
                        + [pltpu.VMEM((qb, 2 * V_HEAD), F32)] * nhs
                        + [pltpu.VMEM((qb, 4 * V_HEAD), F32)] * nhs),
        compiler_params=_cparams(("arbitrary", "arbitrary")),
        name="b2_attention",
    )(q, k, v)


B3_TILES = 4


def _b3_body(*refs, tm):
    o_refs = refs[0:2 * B3_TILES:2]
    h_refs = refs[1:2 * B3_TILES:2]
    wo_ref, post_ref, mpre_ref, rw_ref, rb_ref, h3_ref, xn_ref, route_ref, cnt_ref, cnt_s = refs[2 * B3_TILES:]
    tiles = range(B3_TILES)

    @pl.when(pl.program_id(0) == 0)
    def _():
        cnt_s[...] = jnp.zeros_like(cnt_s)

    mix = [_dot(o_refs[t][0], wo_ref[...]) for t in tiles]
    h3 = [h_refs[t][0] + _rms(mix[t], post_ref[...]) for t in tiles]
    xn = [_rms(h3[t], mpre_ref[...]) for t in tiles]
    logits = [_dot(xn[t].astype(BF16), rw_ref[...]) + rb_ref[...] for t in tiles]
    lane = lax.broadcasted_iota(I32, logits[0].shape, 1).astype(F32)
    big = float(LANES)
    count = cnt_s[...]
    for t in tiles:
        rows = slice(t * tm, (t + 1) * tm)
        h3_ref[rows, :] = h3[t]
        xn_ref[rows, :] = xn[t]
        m1 = jnp.max(logits[t], axis=-1, keepdims=True)
        i1 = jnp.min(jnp.where(logits[t] == m1, lane, big), axis=-1, keepdims=True)
        rest = jnp.where(lane == i1, -jnp.inf, logits[t])
        m2 = jnp.max(rest, axis=-1, keepdims=True)
        i2 = jnp.min(jnp.where(rest == m2, lane, big), axis=-1, keepdims=True)
        e2 = jnp.exp(m2 - m1)
        g1 = 1.0 / (1.0 + e2)
        g2 = e2 / (1.0 + e2)
        route_ref[rows, :] = jnp.where(lane == 0.0, g1, jnp.where(lane == 1.0, g2,
                                       jnp.where(lane == 2.0, i1, jnp.where(lane == 3.0, i2, 0.0))))
        onehot = jnp.where((lane == i1) | (lane == i2), 1.0, 0.0)
        count = count + jnp.sum(onehot, axis=0, keepdims=True)
    cnt_s[...] = count
    cnt_ref[...] = jnp.broadcast_to(count, cnt_ref.shape)


def _b3(o, h2, w_o, post, mpre, rw, rb, *, tm, blk0, nper):
    bsz, _, d = h2.shape
    nblk = bsz * nper
    assert nblk % B3_TILES == 0
    src = lambda t: (lambda i: ((B3_TILES * i + t) // nper, blk0 + (B3_TILES * i + t) % nper, 0))
    dst = lambda i: (i, 0)
    fixed = lambda i: (0, 0)
    rows = nblk * tm
    step = B3_TILES * tm
    tile_specs = []
    for t in range(B3_TILES):
        tile_specs += [pl.BlockSpec((1, tm, o.shape[2]), src(t)), pl.BlockSpec((1, tm, d), src(t))]
    return pl.pallas_call(
        functools.partial(_b3_body, tm=tm),
        grid=(nblk // B3_TILES,),
        in_specs=tile_specs + [pl.BlockSpec(w_o.shape, fixed), pl.BlockSpec((1, d), fixed),
                               pl.BlockSpec((1, d), fixed), pl.BlockSpec(rw.shape, fixed),
                               pl.BlockSpec((1, LANES), fixed)],
        out_specs=[pl.BlockSpec((step, d), dst), pl.BlockSpec((step, d), dst), pl.BlockSpec((step, LANES), dst),
                   pl.BlockSpec((8, LANES), fixed)],
        out_shape=[jax.ShapeDtypeStruct((rows, d), F32), jax.ShapeDtypeStruct((rows, d), F32),
                   jax.ShapeDtypeStruct((rows, LANES), F32), jax.ShapeDtypeStruct((8, LANES), F32)],
        scratch_shapes=[pltpu.VMEM((1, LANES), F32)],
        compiler_params=_cparams(("arbitrary",)),
        name="b3_oproj_router",
    )(*([o, h2] * B3_TILES), w_o, post, mpre, rw, rb)


B4_TILES = 4


def _b4_body(route_ref, cnt_ref, meta_ref, run_s, off_s, *, tile, tm):
    lane = lax.broadcasted_iota(I32, (tm, LANES), 1).astype(F32)

    @pl.when(pl.program_id(0) == 0)
    def _():
        total = cnt_ref[0:1, :]
        ntiles = jnp.floor((total + (tile - 1)) * (1.0 / tile))
        rr = lax.broadcasted_iota(I32, (LANES, LANES), 0)
        cc = lax.broadcasted_iota(I32, (LANES, LANES), 1)
        upper = jnp.where(rr < cc, 1.0, 0.0).astype(BF16)
        before = _dot(jnp.broadcast_to(ntiles, (8, LANES)).astype(BF16), upper)
        off_s[...] = before[0:1, :] * float(tile)
        run_s[...] = jnp.zeros_like(run_s)

    rr = lax.broadcasted_iota(I32, (tm, tm), 0)
    cc = lax.broadcasted_iota(I32, (tm, tm), 1)
    lower = jnp.where(cc < rr, 1.0, 0.0).astype(BF16)
    base = run_s[...] + off_s[...]
    for t in range(B4_TILES):
        route = route_ref[t * tm:(t + 1) * tm, :]
        i1 = jnp.sum(jnp.where(lane == 2.0, route, 0.0), axis=-1, keepdims=True)
        i2 = jnp.sum(jnp.where(lane == 3.0, route, 0.0), axis=-1, keepdims=True)
        oh1 = lane == i1
        oh2 = lane == i2
        onehot = jnp.where(oh1 | oh2, 1.0, 0.0)
        pos = _dot(lower, onehot.astype(BF16)) + base
        p1 = jnp.sum(jnp.where(oh1, pos, 0.0), axis=-1, keepdims=True)
        p2 = jnp.sum(jnp.where(oh2, pos, 0.0), axis=-1, keepdims=True)
        packed = jnp.where(lane == 0.0, p1, jnp.where(lane == 1.0, p2, 0.0))
        meta_ref[t] = jnp.transpose(packed)[0:8, :].astype(I32)
        base = base + jnp.sum(onehot, axis=0, keepdims=True)
    run_s[...] = base - off_s[...]


def _b4(route, cnt, *, tm, tile):
    nblk = route.shape[0] // tm
    assert nblk % B4_TILES == 0
    return pl.pallas_call(
        functools.partial(_b4_body, tile=tile, tm=tm),
        grid=(nblk // B4_TILES,),
        in_specs=[pl.BlockSpec((B4_TILES * tm, LANES), lambda i: (i, 0)),
                  pl.BlockSpec((8, LANES), lambda i: (0, 0))],
        out_specs=pl.BlockSpec((B4_TILES, 8, tm), lambda i: (i, 0, 0)),
        out_shape=jax.ShapeDtypeStruct((nblk, 8, tm), I32),
        scratch_shapes=[pltpu.VMEM((1, LANES), F32), pltpu.VMEM((1, LANES), F32)],
        compiler_params=_cparams(("arbitrary",)),
        name="b4_route_rank",
    )(route, cnt)


def _row(ref, pos):
    return ref.at[pos >> 3, pl.ds(pos & (SUBLANES - 1), 1)]


def _b5_body(meta_ref, xn_ref, xg_in_ref, xg_ref, sem, *, tm):
    del xg_in_ref

    def issue(g, _):
        for u in range(SUBLANES):
            for k in range(TOP_K):
                pos = meta_ref[k * tm + g * SUBLANES + u]
                pltpu.make_async_copy(xn_ref.at[g, pl.ds(u, 1)], _row(xg_ref, pos), sem).start()
        return 0

    lax.fori_loop(0, tm // SUBLANES, issue, 0)
    for k in range(TOP_K):
        pltpu.make_async_copy(xn_ref, xg_ref.at[pl.ds(0, tm // SUBLANES)], sem).wait()


def _b5(meta, xn, xg0, *, tm):
    nblk = meta.shape[0] // (TOP_K * tm)
    return pl.pallas_call(
        functools.partial(_b5_body, tm=tm),
        grid=(nblk,),
        in_specs=[pl.BlockSpec((TOP_K * tm,), lambda i: (i,), memory_space=pltpu.SMEM),
                  pl.BlockSpec((tm // SUBLANES,) + xn.shape[1:], lambda i: (i, 0, 0)),
                  pl.BlockSpec(memory_space=pl.ANY)],
        out_specs=pl.BlockSpec(memory_space=pl.ANY),
        scratch_shapes=[pltpu.SemaphoreType.DMA(())],
        out_shape=jax.ShapeDtypeStruct(xg0.shape, xg0.dtype),
        input_output_aliases={2: 0},
        compiler_params=_cparams(("arbitrary",)),
        name="b5_dispatch",
    )(meta, xn, xg0)


def _b6_body(te_ref, tv_ref, x_ref, wg_ref, wu_ref, wd_ref, y_ref, xb_s, acc_s):
    j = pl.program_id(0)
    f = pl.program_id(1)

    @pl.when(f == 0)
    def _():
        xb_s[...] = x_ref[...].astype(BF16)
        acc_s[...] = jnp.zeros_like(acc_s)

    @pl.when(tv_ref[j] > 0)
    def _():
        xb = xb_s[...]
        g = _dot(xb, wg_ref[0])
        u = _dot(xb, wu_ref[0])
        acc_s[...] += _dot((g * _sigmoid(g) * u).astype(BF16), wd_ref[0])

    @pl.when(f == pl.num_programs(1) - 1)
    def _():
        y_ref[...] = acc_s[...]


def _b6(te, tv, xg, w_gu, w_d, *, tm, tf):
    rg, d = xg.shape
    eff = w_d.shape[1]
    nf = eff // tf
    nt = rg // tm
    return pl.pallas_call(
        _b6_body,
        grid_spec=pltpu.PrefetchScalarGridSpec(
            num_scalar_prefetch=2,
            grid=(nt, nf),
            in_specs=[pl.BlockSpec((tm, d), lambda j, f, te, tv: (j, 0)),
                      pl.BlockSpec((1, d, tf), lambda j, f, te, tv: (te[j], 0, f)),
                      pl.BlockSpec((1, d, tf), lambda j, f, te, tv: (te[j], 0, nf + f)),
                      pl.BlockSpec((1, tf, d), lambda j, f, te, tv: (te[j], f, 0))],
            out_specs=pl.BlockSpec((tm, d), lambda j, f, te, tv: (j, 0)),
            scratch_shapes=[pltpu.VMEM((tm, d), BF16), pltpu.VMEM((tm, d), F32)]),
        out_shape=jax.ShapeDtypeStruct((rg, d), F32),
        compiler_params=_cparams(("arbitrary", "arbitrary")),
        name="b6_moe_ffn",
    )(te, tv, xg, w_gu, w_gu, w_d)


def _b7_body(meta_ref, meta_next_ref, route_ref, h_ref, post_ref, yg_ref, out_ref, buf_s, sem, *, tm):
    i = pl.program_id(0)
    slot = i % 2

    def gather_tile(mref, s):
        def issue(g, _):
            for u in range(SUBLANES):
                for k in range(TOP_K):
                    pos = mref[k * tm + g * SUBLANES + u]
                    pltpu.make_async_copy(_row(yg_ref, pos), buf_s.at[s, k, g, pl.ds(u, 1)], sem.at[s]).start()
            return 0

        lax.fori_loop(0, tm // SUBLANES, issue, 0)

    @pl.when(i == 0)
    def _():
        gather_tile(meta_ref, 0)

    @pl.when(i + 1 < pl.num_programs(0))
    def _():
        gather_tile(meta_next_ref, 1 - slot)

    for k in range(TOP_K):
        pltpu.make_async_copy(yg_ref.at[pl.ds(0, tm // SUBLANES)], buf_s.at[slot, k], sem.at[slot]).wait()
    route = route_ref[...]
    d = out_ref.shape[1]
    y = route[:, 0:1] * buf_s[slot, 0].reshape(tm, d) + route[:, 1:2] * buf_s[slot, 1].reshape(tm, d)
    out_ref[...] = h_ref[...] + _rms(y, post_ref[...])


def _b7(meta, route, h3, post, yg, *, tm):
    nblk = meta.shape[0] // (TOP_K * tm)
    d = h3.shape[1]
    row = lambda i: (i, 0)
    return pl.pallas_call(
        functools.partial(_b7_body, tm=tm),
        grid=(nblk,),
        in_specs=[pl.BlockSpec((TOP_K * tm,), lambda i: (i,), memory_space=pltpu.SMEM),
                  pl.BlockSpec((TOP_K * tm,), lambda i: (jnp.minimum(i + 1, nblk - 1),),
                               memory_space=pltpu.SMEM),
                  pl.BlockSpec((tm, LANES), row), pl.BlockSpec((tm, d), row),
                  pl.BlockSpec((1, d), lambda i: (0, 0)),
                  pl.BlockSpec(memory_space=pl.ANY)],
        out_specs=pl.BlockSpec((tm, d), row),
        scratch_shapes=[pltpu.VMEM((2, TOP_K, tm // SUBLANES, SUBLANES, d), F32),
                        pltpu.SemaphoreType.DMA((2,))],
        out_shape=jax.ShapeDtypeStruct((nblk * tm, d), F32),
        compiler_params=_cparams(("arbitrary",)),
        name="b7_combine",
    )(meta, meta, route, h3, post, yg)


def _pick(n, target, mult):
    best = mult
    for cand in range(mult, min(n, target) + 1, mult):
        if n % cand == 0:
            best = cand
    return best


def kernel(x, meta_tokens, a_pre_norm, a_w_in, a_conv_w, a_conv_b, a_gate_a_w, a_gate_a_b, a_gate_x_w, a_gate_x_b, a_lambda, a_w_out, a_post_norm, kv_norm, kv_w_dkv, kv_latent_norm, kv_w_ukv, b_pre_norm, b_w_dq, b_q_norm, b_w_uq, b_w_o, b_post_norm, f_pre_norm, f_w_gate_up, f_w_down, f_post_norm, m_pre_norm, m_router_w, m_router_b, m_w_gate_up, m_w_down, m_post_norm):
    bsz, seq, d = x.shape
    n_meta = meta_tokens.shape[0]
    assert a_pre_norm.shape[0] == 1 and b_pre_norm.shape[0] == 1, "one RG-LRU layer + one MLA layer"
    t_all = n_meta + seq
    assert t_all % TT == 0 and seq % ATT_BLK == 0 and bsz % 8 == 0 and n_meta % 16 == 0
    rows = t_all * bsz
    row = lambda v: v.reshape(1, -1).astype(F32)
    bf = lambda w: w.astype(BF16)

    h0 = jnp.concatenate([jnp.broadcast_to(meta_tokens.astype(F32)[:, None, :], (n_meta, bsz, d)),
                          jnp.transpose(x, (1, 0, 2))], axis=0).reshape(rows, d)

    wg = bf(jnp.concatenate([a_gate_a_w[0], a_gate_x_w[0]], axis=-1))
    h1 = _a2(h0, row(a_pre_norm[0]), bf(a_w_in[0]), a_conv_w[0].astype(F32), row(a_conv_b[0]), wg,
             row(a_gate_a_b[0]), row(a_gate_x_b[0]), row(a_lambda[0]), bf(a_w_out[0]), row(a_post_norm[0]),
             tt=TT_SMALL, bsz=bsz)
    d_ff = f_w_down.shape[1]
    h2 = _a3(h1, row(f_pre_norm[0]), bf(f_w_gate_up[0]), bf(f_w_down[0]), row(f_post_norm[0]),
             bsz=bsz, tt=TT, npad=PAD_T // TT, tf=_pick(d_ff, 1024, LANES))
    t_pad = PAD_T + t_all
    seq0 = PAD_T + n_meta
    assert seq0 % ATT_BLK == 0 and t_pad % (TT_SMALL * bsz) == 0

    kv_lora = kv_latent_norm.shape[0]
    nh = b_w_uq.shape[2] // (QK_NOPE + QK_ROPE)
    hr = QK_ROPE // 2
    scale = (QK_NOPE + QK_ROPE) ** -0.5 * math.log2(math.e)
    inv = ROPE_THETA ** (-jnp.arange(0, QK_ROPE, 2, dtype=F32) / QK_ROPE)
    pos = jnp.maximum(jnp.arange(t_pad, dtype=F32) - PAD_T, 0.0)
    ang = pos[:, None] * inv[None, :]
    cos, sin = jnp.cos(ang), jnp.sin(ang)
    one = jnp.ones((t_pad, QK_NOPE), F32)
    zero = jnp.zeros((t_pad, QK_NOPE), F32)
    tq = scale * jnp.concatenate([one, cos, cos, -sin, sin], axis=1)
    tkc = jnp.concatenate([zero, cos, cos, cos, cos], axis=1)
    tks = jnp.concatenate([zero, -sin, sin, -sin, sin], axis=1)

    w_c = kv_w_dkv[:, :kv_lora]
    k1 = kv_w_dkv[:, kv_lora:kv_lora + hr]
    k2 = kv_w_dkv[:, kv_lora + hr:]
    zk = jnp.zeros((d, QK_NOPE), F32)
    wka = jnp.concatenate([zk, k1, k2, k1, k2], axis=1)
    wkb = jnp.concatenate([zk, k2, k1, k2, k1], axis=1)
    ukv = kv_w_ukv.reshape(kv_lora, nh, QK_NOPE + V_HEAD)
    wk = jnp.concatenate([ukv[:, :, :QK_NOPE], jnp.zeros((kv_lora, nh, HEAD_PAD - QK_NOPE), F32)],
                         axis=2).reshape(kv_lora, nh * HEAD_PAD)
    wv = ukv[:, :, QK_NOPE:].reshape(kv_lora, nh * V_HEAD)
    uq = b_w_uq[0].reshape(-1, nh, QK_NOPE + QK_ROPE)
    qn_, q1, q2 = uq[:, :, :QK_NOPE], uq[:, :, QK_NOPE:QK_NOPE + hr], uq[:, :, QK_NOPE + hr:]
    wq = jnp.concatenate([qn_, q1, q2, q2, q1], axis=2).reshape(-1, nh * HEAD_PAD)
    q, k, v = _b1(h2.reshape(bsz * t_pad, d), row(kv_norm), row(b_pre_norm[0]), bf(w_c), bf(wka), bf(wkb),
                  row(kv_latent_norm), bf(wk), bf(wv), bf(b_w_dq[0]), row(b_q_norm[0]), bf(wq), tq, tkc, tks,
                  tm=TT_SMALL * bsz, nh=nh)

    o = _b2(q.reshape(bsz, t_pad, nh * HEAD_PAD), k.reshape(bsz, t_pad, nh * HEAD_PAD),
            v.reshape(bsz, t_pad, nh * V_HEAD), nh=nh, meta0=PAD_T, seq0=seq0, seq=seq,
            qb=_pick(seq, ATT_QBLK, ATT_BLK), nhs=ATT_HEADS)

    rows_seq = seq * bsz
    tm1 = ATT_BLK
    assert rows_seq % MOE_TILE == 0
    ne = m_router_w.shape[2]
    rw = bf(jnp.concatenate([m_router_w[0], jnp.zeros((d, LANES - ne), F32)], axis=1))
    rb = jnp.concatenate([m_router_b[0].astype(F32), jnp.full((LANES - ne,), -1e30, F32)]).reshape(1, LANES)
    h3, xn, route, cnt = _b3(o, h2, bf(b_w_o[0]), row(b_post_norm[0]), row(m_pre_norm[0]), rw, rb,
                             tm=tm1, blk0=seq0 // tm1, nper=seq // tm1)

    tmg = MOE_TILE
    meta = _b4(route, cnt, tm=tm1, tile=tmg)[:, :TOP_K, :].reshape(-1)
    counts = cnt[0, :ne].astype(I32)
    padded = ((counts + tmg - 1) // tmg) * tmg
    ends = jnp.cumsum(padded)
    nt = (TOP_K * rows_seq) // tmg + ne
    tile_start = jnp.arange(nt, dtype=I32) * tmg
    tv = (tile_start < ends[-1]).astype(I32)
    last = jnp.maximum(ends[-1] // tmg - 1, 0)
    te_raw = jnp.sum((tile_start[:, None] >= ends[None, :]).astype(I32), axis=1)
    te = jnp.minimum(jnp.where(tv > 0, te_raw, te_raw[last]), ne - 1).astype(I32)

    grp = lambda a: a.reshape(a.shape[0] // SUBLANES, SUBLANES, d)
    xg = _b5(meta, grp(xn), jnp.zeros((nt * tmg // SUBLANES, SUBLANES, d), F32), tm=tm1)
    e_ff = m_w_down.shape[2]
    yg = _b6(te, tv, xg.reshape(nt * tmg, d), bf(m_w_gate_up[0]), bf(m_w_down[0]), tm=tmg,
             tf=_pick(e_ff, MOE_FF_CHUNK, LANES))
    h4 = _b7(meta, route, h3, row(m_post_norm[0]), grp(yg), tm=tm1)
    return h4.reshape(bsz, seq, d)
```

```python
import functools
import math

import jax
import jax.numpy as jnp
from jax import lax
from jax.experimental import pallas as pl
from jax.experimental.pallas import tpu as pltpu

F32 = jnp.float32
BF16 = jnp.bfloat16
I32 = jnp.int32

RMS_EPS = 1e-6
RG_C = 8.0
ROPE_THETA = 10000.0
QK_NOPE = 64
QK_ROPE = 32
V_HEAD = 64
HEAD_PAD = 128
TOP_K = 2
LANES = 128
SUBLANES = 8

TT = 48
TT_SMALL = 24
PAD_T = 240
ATT_BLK = 256
ATT_KBLK = 256
ATT_QBLK = 1024
ATT_HEADS = 2
MOE_TILE = 512
MOE_FF_CHUNK = 1792
VMEM_LIMIT = 56 * 1024 * 1024


def _cparams(sem):
    return pltpu.CompilerParams(dimension_semantics=sem, vmem_limit_bytes=VMEM_LIMIT)


def _rms(x, g):
    ms = jnp.mean(x * x, axis=-1, keepdims=True)
    return x * lax.rsqrt(ms + RMS_EPS) * g


def _sigmoid(x):
    return 0.5 * jnp.tanh(0.5 * x) + 0.5


def _dot(a, b):
    return jnp.dot(a, b, preferred_element_type=F32)


def _dot_t(a, b):
    return lax.dot_general(a, b, (((1,), (1,)), ((), ())), preferred_element_type=F32)


def _a2_body(h_ref, hnext_ref, pre_ref, win_ref, cw_ref, cb_ref, wg_ref, gab_ref, gxb_ref, lam_ref,
             wo_ref, pn_ref, out_ref, yx_s, ext_s, a_s, b_s, carry_s, *, nb, tt, bsz, cw):
    tm = tt * bsz
    tail = (cw - 1) * bsz
    c = ext_s.shape[1]
    i = pl.program_id(0)
    slot = i % 2

    def project(href, s):
        xn = _rms(href[...], pre_ref[...]).astype(BF16)
        yx_s[s, :, 0:c] = _dot(xn, win_ref[:, 0:c]).astype(BF16)
        yx_s[s, :, c:] = _dot(xn, win_ref[:, c:]).astype(BF16)

    @pl.when(i == 0)
    def _():
        ext_s[0:tail, :] = jnp.zeros((tail, c), F32)
        carry_s[...] = jnp.zeros_like(carry_s)
        project(h_ref, 0)

    ext_s[tail:tail + tm, :] = yx_s[slot, :, c:].astype(F32)

    xn_next = _rms(hnext_ref[...], pre_ref[...]).astype(BF16)
    pw = 2 * LANES
    pieces = [(s0, min(s0 + pw, 2 * c)) for s0 in range(0, 2 * c, pw)]

    def project_piece(p):
        lo, hi = pieces[p]
        yx_s[1 - slot, :, lo:hi] = _dot(xn_next, win_ref[:, lo:hi]).astype(BF16)

    xc = cb_ref[...] + ext_s[0:tm, :] * cw_ref[0:1, :]
    for k in range(1, cw):
        xc = xc + ext_s[k * bsz:k * bsz + tm, :] * cw_ref[k:k + 1, :]
    ext_s[0:tail, :] = ext_s[tm:tm + tail, :]

    neg_lam = -lam_ref[...]
    sp = jnp.maximum(neg_lam, 0.0) + jnp.log1p(jnp.exp(-jnp.abs(neg_lam)))
    rate = sp * (-RG_C * math.log2(math.e))
    xcb = xc.astype(BF16)
    per_block = -(-len(pieces) // nb)
    for n in range(nb):
        for p in range(n * per_block, min((n + 1) * per_block, len(pieces))):
            project_piece(p)
        sl = slice(n * LANES, (n + 1) * LANES)
        g = _dot(xcb[:, sl], wg_ref[n])
        r = _sigmoid(g[:, :LANES] + gab_ref[:, sl])
        ig = _sigmoid(g[:, LANES:] + gxb_ref[:, sl])
        a = jnp.exp2(r * rate[:, sl])
        a_s[:, sl] = a
        u = 1.0 - a * a
        root = jnp.where(u > 0.0, u * lax.rsqrt(u), 0.0)
        b_s[:, sl] = root * (ig * xc[:, sl])

    def step(t, h):
        rows = pl.ds(pl.multiple_of(t * bsz, bsz), bsz)
        h = a_s[rows, :] * h + b_s[rows, :]
        b_s[rows, :] = h
        return h

    carry_s[...] = lax.fori_loop(0, tt, step, carry_s[...], unroll=4)

    z = (jax.nn.gelu(yx_s[slot, :, 0:c].astype(F32)) * b_s[...]).astype(BF16)
    mix = _dot(z, wo_ref[...])
    out_ref[...] = h_ref[...] + _rms(mix, pn_ref[...])


def _a2(h0, pre, w_in, conv_w, conv_b, wg, ga_b, gx_b, lam, w_out, post, *, tt, bsz):
    r, d = h0.shape
    c = w_in.shape[1] // 2
    nb = wg.shape[0]
    cw = conv_w.shape[0]
    tm = tt * bsz
    nblk = r // tm
    row = lambda i: (i, 0)
    fixed2 = lambda i: (0, 0)
    fixed3 = lambda i: (0, 0, 0)
    return pl.pallas_call(
        functools.partial(_a2_body, nb=nb, tt=tt, bsz=bsz, cw=cw),
        grid=(nblk,),
        in_specs=[pl.BlockSpec((tm, d), row),
                  pl.BlockSpec((tm, d), lambda i: (jnp.minimum(i + 1, nblk - 1), 0)),
                  pl.BlockSpec((1, d), fixed2),
                  pl.BlockSpec((d, 2 * c), fixed2, pipeline_mode=pl.Buffered(1)),
                  pl.BlockSpec((cw, c), fixed2), pl.BlockSpec((1, c), fixed2),
                  pl.BlockSpec((nb, LANES, 2 * LANES), fixed3),
                  pl.BlockSpec((1, c), fixed2), pl.BlockSpec((1, c), fixed2), pl.BlockSpec((1, c), fixed2),
                  pl.BlockSpec((c, d), fixed2, pipeline_mode=pl.Buffered(1)), pl.BlockSpec((1, d), fixed2)],
        out_specs=pl.BlockSpec((tm, d), row),
        out_shape=jax.ShapeDtypeStruct((r, d), F32),
        scratch_shapes=[pltpu.VMEM((2, tm, 2 * c), BF16),
                        pltpu.VMEM((tm + (cw - 1) * bsz, c), F32), pltpu.VMEM((tm, c), F32),
                        pltpu.VMEM((tm, c), F32), pltpu.VMEM((bsz, c), F32)],
        compiler_params=_cparams(("arbitrary",)),
        name="a2_rglru",
    )(h0, h0, pre, w_in, conv_w, conv_b, wg, ga_b, gx_b, lam, w_out, post)


def _a3_body(h_ref, pre_ref, wgu_ref, wd_ref, post_ref, out_ref, tr_s, *, npad, bsz, tt, tf):
    i = pl.program_id(0)

    @pl.when(i < npad)
    def _():
        out_ref[...] = jnp.zeros_like(out_ref)

    @pl.when(i >= npad)
    def _():
        h = h_ref[...]
        xn = _rms(h, pre_ref[...]).astype(BF16)
        ff = wd_ref.shape[0]
        acc = None
        for c in range(ff // tf):
            g = _dot(xn, wgu_ref[:, c * tf:(c + 1) * tf])
            u = _dot(xn, wgu_ref[:, ff + c * tf:ff + (c + 1) * tf])
            part = _dot((g * _sigmoid(g) * u).astype(BF16), wd_ref[c * tf:(c + 1) * tf, :])
            acc = part if acc is None else acc + part
        res = h + _rms(acc, post_ref[...])
        for c in range(tr_s.shape[0]):
            tr_s[c] = res[:, c * LANES:(c + 1) * LANES]
        for b in range(bsz):
            for c in range(tr_s.shape[0]):
                out_ref[b, :, c * LANES:(c + 1) * LANES] = tr_s[c, pl.ds(b, tt, stride=bsz), :]


def _a3(h1, pre, w_gu, w_d, post, *, bsz, tt, npad, tf):
    r, d = h1.shape
    tm = tt * bsz
    nblk = r // tm
    fixed = lambda i: (0, 0)
    resident = lambda a: pl.BlockSpec(a.shape, fixed, pipeline_mode=pl.Buffered(1))
    return pl.pallas_call(
        functools.partial(_a3_body, npad=npad, bsz=bsz, tt=tt, tf=tf),
        grid=(npad + nblk,),
        in_specs=[pl.BlockSpec((tm, d), lambda i: (jnp.maximum(i - npad, 0), 0)), pl.BlockSpec((1, d), fixed),
                  resident(w_gu), resident(w_d), pl.BlockSpec((1, d), fixed)],
        out_specs=pl.BlockSpec((bsz, tt, d), lambda i: (0, i, 0)),
        out_shape=jax.ShapeDtypeStruct((bsz, (npad + nblk) * tt, d), F32),
        scratch_shapes=[pltpu.VMEM((d // LANES, tm, LANES), F32)],
        compiler_params=_cparams(("arbitrary",)),
        name="a3_dense_ffn",
    )(h1, pre, w_gu, w_d, post)


def _b1_body(h_ref, kvn_ref, bpre_ref, wkv_ref, latn_ref, wk_ref, wv_ref,
             wdq_ref, qn_ref, wq_ref, tq_ref, tkc_ref, tks_ref, q_ref, k_ref, v_ref, *, nh):
    h = h_ref[...]
    hn = h * lax.rsqrt(jnp.mean(h * h, axis=-1, keepdims=True) + RMS_EPS)
    ukv = (hn * kvn_ref[...]).astype(BF16)
    uq = (hn * bpre_ref[...]).astype(BF16)

    lat = wkv_ref.shape[1] - 2 * HEAD_PAD
    dkv = _dot(ukv, wkv_ref[...])
    ckv = _rms(dkv[:, :lat], latn_ref[...]).astype(BF16)
    kr = dkv[:, lat:lat + HEAD_PAD] * tkc_ref[...] + dkv[:, lat + HEAD_PAD:] * tks_ref[...]
    kn = _dot(ckv, wk_ref[...])
    for hd in range(nh):
        sl = slice(hd * HEAD_PAD, (hd + 1) * HEAD_PAD)
        k_ref[:, sl] = (kn[:, sl] + kr).astype(BF16)
    v_ref[...] = _dot(ckv, wv_ref[...]).astype(BF16)

    cq = _rms(_dot(uq, wdq_ref[...]), qn_ref[...]).astype(BF16)
    qr = _dot(cq, wq_ref[...])
    tq = tq_ref[...]
    for hd in range(nh):
        sl = slice(hd * HEAD_PAD, (hd + 1) * HEAD_PAD)
        q_ref[:, sl] = (qr[:, sl] * tq).astype(BF16)


def _b1(h2, kvn, bpre, wkv, latn, wk, wv, wdq, qn, wq, tq, tkc, tks, *, tm, nh):
    r, d = h2.shape
    nper = tq.shape[0] // tm
    row = lambda i: (i, 0)
    tab = lambda i: (i % nper, 0)
    fixed = lambda i: (0, 0)
    full = lambda a: pl.BlockSpec(a.shape, fixed)
    return pl.pallas_call(
        functools.partial(_b1_body, nh=nh),
        grid=(r // tm,),
        in_specs=[pl.BlockSpec((tm, d), row), full(kvn), full(bpre), full(wkv),
                  full(latn), full(wk), full(wv), full(wdq), full(qn), full(wq),
                  pl.BlockSpec((tm, HEAD_PAD), tab), pl.BlockSpec((tm, HEAD_PAD), tab),
                  pl.BlockSpec((tm, HEAD_PAD), tab)],
        out_specs=[pl.BlockSpec((tm, nh * HEAD_PAD), row), pl.BlockSpec((tm, nh * HEAD_PAD), row),
                   pl.BlockSpec((tm, nh * V_HEAD), row)],
        out_shape=[jax.ShapeDtypeStruct((r, nh * HEAD_PAD), BF16),
                   jax.ShapeDtypeStruct((r, nh * HEAD_PAD), BF16),
                   jax.ShapeDtypeStruct((r, nh * V_HEAD), BF16)],
        compiler_params=_cparams(("arbitrary",)),
        name="b1_qkv_proj",
    )(h2, kvn, bpre, wkv, latn, wk, wv, wdq, qn, wq, tq, tkc, tks)


def _tile2(x):
    return jnp.concatenate([x, x], axis=1)


def _b2_body(q_ref, k_ref, v_ref, o_ref, *scratch, meta0, seq0, nq, qb, kb, nhs):
    pair = 2 * V_HEAD
    t_pad = v_ref.shape[1]
    nsub = qb // kb
    npairs = nhs // 2
    vext_s = scratch[0:npairs]
    m_s = scratch[npairs:npairs + nhs]
    acc_s = scratch[npairs + nhs:npairs + 2 * nhs]
    for pp in range(npairs):
        vext_s[pp][meta0:, 0:pair] = v_ref[0, meta0:, pp * pair:(pp + 1) * pair]
        vext_s[pp][meta0:, pair:] = jnp.ones((t_pad - meta0, pair), BF16)
    o_ref[0, 0:seq0, :] = jnp.zeros((seq0, npairs * pair), o_ref.dtype)
    lane = lax.broadcasted_iota(I32, (qb, pair), 1)
    neg = jnp.finfo(F32).min
    heads = range(nhs)

    def scores(hh, qh, qlo, krows):
        return _dot_t(qh[qlo:], k_ref[0, krows, hh * HEAD_PAD:(hh + 1) * HEAD_PAD])

    def absorb(hh, s, qlo, krows, masked):
        if masked:
            rowi = lax.broadcasted_iota(I32, s.shape, 0)
            coli = lax.broadcasted_iota(I32, s.shape, 1)
            s = jnp.where(coli <= rowi, s, neg)
        m_old = m_s[hh][qlo:, :]
        m_new = jnp.maximum(m_old, jnp.max(s, axis=-1, keepdims=True))
        alpha = jnp.exp2(m_old - m_new)
        p = jnp.exp2(s - jnp.concatenate([m_new] * (s.shape[1] // pair), axis=1))
        acc_s[hh][qlo:, :] = (_tile2(alpha) * acc_s[hh][qlo:, :]
                              + _dot(p.astype(BF16), vext_s[hh // 2][krows, :]))
        m_s[hh][qlo:, :] = m_new

    def qblock(i, _):
        q0 = pl.multiple_of(seq0 + i * qb, ATT_BLK)
        qs = [q_ref[0, pl.ds(q0, qb), hh * HEAD_PAD:(hh + 1) * HEAD_PAD] for hh in heads]
        for hh in heads:
            s = _dot_t(qs[hh], k_ref[0, meta0:seq0, hh * HEAD_PAD:(hh + 1) * HEAD_PAD])
            m = jnp.max(s, axis=-1, keepdims=True)
            p = jnp.exp2(s - m)
            acc_s[hh][...] = _dot(p.astype(BF16), vext_s[hh // 2][meta0:seq0, :])
            m_s[hh][...] = jnp.broadcast_to(m, (qb, pair))

        def kv_step(j, _):
            krows = pl.ds(pl.multiple_of(seq0 + j * kb, ATT_BLK), kb)
            for hh in heads:
                absorb(hh, scores(hh, qs[hh], 0, krows), 0, krows, False)
            return 0

        lax.fori_loop(0, i * nsub, kv_step, 0)
        for jj in range(nsub):
            krows = pl.ds(pl.multiple_of(q0 + jj * kb, ATT_BLK), kb)
            for hh in heads:
                absorb(hh, scores(hh, qs[hh], jj * kb, krows), jj * kb, krows, True)
        for pp in range(npairs):
            res = []
            for hh in (2 * pp, 2 * pp + 1):
                acc = acc_s[hh][...]
                res.append(acc[:, :pair] / acc[:, pair:])
            o_ref[0, pl.ds(q0, qb), pp * pair:(pp + 1) * pair] = jnp.where(
                lane < V_HEAD, res[0], res[1]).astype(o_ref.dtype)
        return 0

    lax.fori_loop(0, nq, qblock, 0)


def _b2(q, k, v, *, nh, meta0, seq0, seq, qb, nhs):
    bsz, t_pad, _ = q.shape
    qk = pl.BlockSpec((1, t_pad, nhs * HEAD_PAD), lambda b, p: (b, 0, p))
    vo = pl.BlockSpec((1, t_pad, nhs * V_HEAD), lambda b, p: (b, 0, p))
    return pl.pallas_call(
        functools.partial(_b2_body, meta0=meta0, seq0=seq0, nq=seq // qb, qb=qb, kb=min(ATT_KBLK, qb), nhs=nhs),
        grid=(bsz, nh // nhs),
        in_specs=[qk, qk, vo],
        out_specs=vo,
        out_shape=jax.ShapeDtypeStruct(v.shape, BF16),
        scratch_shapes=([pltpu.VMEM((t_pad, 4 * V_HEAD), BF16)] * (nhs // 2)
                        + [pltpu.VMEM((qb, 2 * V_HEAD), F32)] * nhs
                        + [pltpu.VMEM((qb, 4 * V_HEAD), F32)] * nhs),
        compiler_params=_cparams(("arbitrary", "arbitrary")),
        name="b2_attention",
    )(q, k, v)


B3_TILES = 4


def _b3_body(*refs, tm):
    o_refs = refs[0:2 * B3_TILES:2]
    h_refs = refs[1:2 * B3_TILES:2]
    wo_ref, post_ref, mpre_ref, rw_ref, rb_ref, h3_ref, xn_ref, route_ref, cnt_ref, cnt_s = refs[2 * B3_TILES:]
    tiles = range(B3_TILES)

    @pl.when(pl.program_id(0) == 0)
    def _():
        cnt_s[...] = jnp.zeros_like(cnt_s)

    mix = [_dot(o_refs[t][0], wo_ref[...]) for t in tiles]
    h3 = [h_refs[t][0] + _rms(mix[t], post_ref[...]) for t in tiles]
    xn = [_rms(h3[t], mpre_ref[...]) for t in tiles]
    logits = [_dot(xn[t].astype(BF16), rw_ref[...]) + rb_ref[...] for t in tiles]
    lane = lax.broadcasted_iota(I32, logits[0].shape, 1).astype(F32)
    big = float(LANES)
    count = cnt_s[...]
    for t in tiles:
        rows = slice(t * tm, (t + 1) * tm)
        h3_ref[rows, :] = h3[t]
        xn_ref[rows, :] = xn[t]
        m1 = jnp.max(logits[t], axis=-1, keepdims=True)
        i1 = jnp.min(jnp.where(logits[t] == m1, lane, big), axis=-1, keepdims=True)
        rest = jnp.where(lane == i1, -jnp.inf, logits[t])
        m2 = jnp.max(rest, axis=-1, keepdims=True)
        i2 = jnp.min(jnp.where(rest == m2, lane, big), axis=-1, keepdims=True)
        e2 = jnp.exp(m2 - m1)
        g1 = 1.0 / (1.0 + e2)
        g2 = e2 / (1.0 + e2)
        route_ref[rows, :] = jnp.where(lane == 0.0, g1, jnp.where(lane == 1.0, g2,
                                       jnp.where(lane == 2.0, i1, jnp.where(lane == 3.0, i2, 0.0))))
        onehot = jnp.where((lane == i1) | (lane == i2), 1.0, 0.0)
        count = count + jnp.sum(onehot, axis=0, keepdims=True)
    cnt_s[...] = count
    cnt_ref[...] = jnp.broadcast_to(count, cnt_ref.shape)


def _b3(o, h2, w_o, post, mpre, rw, rb, *, tm, blk0, nper):
    bsz, _, d = h2.shape
    nblk = bsz * nper
    assert nblk % B3_TILES == 0
    src = lambda t: (lambda i: ((B3_TILES * i + t) // nper, blk0 + (B3_TILES * i + t) % nper, 0))
    dst = lambda i: (i, 0)
    fixed = lambda i: (0, 0)
    rows = nblk * tm
    step = B3_TILES * tm
    tile_specs = []
    for t in range(B3_TILES):
        tile_specs += [pl.BlockSpec((1, tm, o.shape[2]), src(t)), pl.BlockSpec((1, tm, d), src(t))]
    return pl.pallas_call(
        functools.partial(_b3_body, tm=tm),
        grid=(nblk // B3_TILES,),
        in_specs=tile_specs + [pl.BlockSpec(w_o.shape, fixed), pl.BlockSpec((1, d), fixed),
                               pl.BlockSpec((1, d), fixed), pl.BlockSpec(rw.shape, fixed),
                               pl.BlockSpec((1, LANES), fixed)],
        out_specs=[pl.BlockSpec((step, d), dst), pl.BlockSpec((step, d), dst), pl.BlockSpec((step, LANES), dst),
                   pl.BlockSpec((8, LANES), fixed)],
        out_shape=[jax.ShapeDtypeStruct((rows, d), F32), jax.ShapeDtypeStruct((rows, d), F32),
                   jax.ShapeDtypeStruct((rows, LANES), F32), jax.ShapeDtypeStruct((8, LANES), F32)],
        scratch_shapes=[pltpu.VMEM((1, LANES), F32)],
        compiler_params=_cparams(("arbitrary",)),
        name="b3_oproj_router",
    )(*([o, h2] * B3_TILES), w_o, post, mpre, rw, rb)


B4_TILES = 4


def _b4_body(route_ref, cnt_ref, meta_ref, run_s, off_s, *, tile, tm):
    lane = lax.broadcasted_iota(I32, (tm, LANES), 1).astype(F32)

    @pl.when(pl.program_id(0) == 0)
    def _():
        total = cnt_ref[0:1, :]
        ntiles = jnp.floor((total + (tile - 1)) * (1.0 / tile))
        rr = lax.broadcasted_iota(I32, (LANES, LANES), 0)
        cc = lax.broadcasted_iota(I32, (LANES, LANES), 1)
        upper = jnp.where(rr < cc, 1.0, 0.0).astype(BF16)
        before = _dot(jnp.broadcast_to(ntiles, (8, LANES)).astype(BF16), upper)
        off_s[...] = before[0:1, :] * float(tile)
        run_s[...] = jnp.zeros_like(run_s)

    rr = lax.broadcasted_iota(I32, (tm, tm), 0)
    cc = lax.broadcasted_iota(I32, (tm, tm), 1)
    lower = jnp.where(cc < rr, 1.0, 0.0).astype(BF16)
    base = run_s[...] + off_s[...]
    for t in range(B4_TILES):
        route = route_ref[t * tm:(t + 1) * tm, :]
        i1 = jnp.sum(jnp.where(lane == 2.0, route, 0.0), axis=-1, keepdims=True)
        i2 = jnp.sum(jnp.where(lane == 3.0, route, 0.0), axis=-1, keepdims=True)
        oh1 = lane == i1
        oh2 = lane == i2
        onehot = jnp.where(oh1 | oh2, 1.0, 0.0)
        pos = _dot(lower, onehot.astype(BF16)) + base
        p1 = jnp.sum(jnp.where(oh1, pos, 0.0), axis=-1, keepdims=True)
        p2 = jnp.sum(jnp.where(oh2, pos, 0.0), axis=-1, keepdims=True)
        packed = jnp.where(lane == 0.0, p1, jnp.where(lane == 1.0, p2, 0.0))
        meta_ref[t] = jnp.transpose(packed)[0:8, :].astype(I32)
        base = base + jnp.sum(onehot, axis=0, keepdims=True)
    run_s[...] = base - off_s[...]


def _b4(route, cnt, *, tm, tile):
    nblk = route.shape[0] // tm
    assert nblk % B4_TILES == 0
    return pl.pallas_call(
        functools.partial(_b4_body, tile=tile, tm=tm),
        grid=(nblk // B4_TILES,),
        in_specs=[pl.BlockSpec((B4_TILES * tm, LANES), lambda i: (i, 0)),
                  pl.BlockSpec((8, LANES), lambda i: (0, 0))],
        out_specs=pl.BlockSpec((B4_TILES, 8, tm), lambda i: (i, 0, 0)),
        out_shape=jax.ShapeDtypeStruct((nblk, 8, tm), I32),
        scratch_shapes=[pltpu.VMEM((1, LANES), F32), pltpu.VMEM((1, LANES), F32)],
        compiler_params=_cparams(("arbitrary",)),
        name="b4_route_rank",
    )(route, cnt)


def _row(ref, pos):
    return ref.at[pos >> 3, pl.ds(pos & (SUBLANES - 1), 1)]


DISPATCH_BUFS = 3


def _b5_body(meta_ref, xn_ref, xg_in_ref, xg_ref, buf_s, in_sem, out_sem, *, tm):
    del xg_in_ref
    i = pl.program_id(0)
    n = pl.num_programs(0)
    groups = tm // SUBLANES
    slot = i % DISPATCH_BUFS
    nxt = (i + 1) % DISPATCH_BUFS

    def fetch(t, s):
        return pltpu.make_async_copy(xn_ref.at[pl.ds(t * groups, groups)], buf_s.at[s], in_sem.at[s])

    def drain(s):
        for k in range(TOP_K):
            pltpu.make_async_copy(buf_s.at[s], xg_ref.at[pl.ds(0, groups)], out_sem.at[s]).wait()

    @pl.when(i == 0)
    def _():
        fetch(0, 0).start()

    @pl.when(i >= DISPATCH_BUFS - 1)
    def _():
        drain(nxt)

    @pl.when(i + 1 < n)
    def _():
        fetch(i + 1, nxt).start()

    fetch(i, slot).wait()

    def issue(g, _):
        for u in range(SUBLANES):
            for k in range(TOP_K):
                pos = meta_ref[k * tm + g * SUBLANES + u]
                pltpu.make_async_copy(buf_s.at[slot, g, pl.ds(u, 1)], _row(xg_ref, pos), out_sem.at[slot]).start()
        return 0

    lax.fori_loop(0, groups, issue, 0)

    @pl.when(i == n - 1)
    def _():
        for back in range(DISPATCH_BUFS - 1):
            @pl.when(i >= back)
            def _():
                drain((i - back) % DISPATCH_BUFS)


def _b5(meta, xn, xg0, *, tm):
    nblk = meta.shape[0] // (TOP_K * tm)
    return pl.pallas_call(
        functools.partial(_b5_body, tm=tm),
        grid=(nblk,),
        in_specs=[pl.BlockSpec((TOP_K * tm,), lambda i: (i,), memory_space=pltpu.SMEM),
                  pl.BlockSpec(memory_space=pl.ANY),
                  pl.BlockSpec(memory_space=pl.ANY)],
        out_specs=pl.BlockSpec(memory_space=pl.ANY),
        scratch_shapes=[pltpu.VMEM((DISPATCH_BUFS, tm // SUBLANES) + xn.shape[1:], xn.dtype),
                        pltpu.SemaphoreType.DMA((DISPATCH_BUFS,)),
                        pltpu.SemaphoreType.DMA((DISPATCH_BUFS,))],
        out_shape=jax.ShapeDtypeStruct(xg0.shape, xg0.dtype),
        input_output_aliases={2: 0},
        compiler_params=_cparams(("arbitrary",)),
        name="b5_dispatch",
    )(meta, xn, xg0)


def _b6_body(te_ref, tv_ref, x_ref, wg_ref, wu_ref, wd_ref, y_ref, xb_s, acc_s):
    j = pl.program_id(0)
    f = pl.program_id(1)

    @pl.when(f == 0)
    def _():
        xb_s[...] = x_ref[...].astype(BF16)
        acc_s[...] = jnp.zeros_like(acc_s)

    @pl.when(tv_ref[j] > 0)
    def _():
        xb = xb_s[...]
        g = _dot(xb, wg_ref[0])
        u = _dot(xb, wu_ref[0])
        acc_s[...] += _dot((g * _sigmoid(g) * u).astype(BF16), wd_ref[0])

    @pl.when(f == pl.num_programs(1) - 1)
    def _():
        y_ref[...] = acc_s[...]


def _b6(te, tv, xg, w_gu, w_d, *, tm, tf):
    rg, d = xg.shape
    eff = w_d.shape[1]
    nf = eff // tf
    nt = rg // tm
    return pl.pallas_call(
        _b6_body,
        grid_spec=pltpu.PrefetchScalarGridSpec(
            num_scalar_prefetch=2,
            grid=(nt, nf),
            in_specs=[pl.BlockSpec((tm, d), lambda j, f, te, tv: (j, 0)),
                      pl.BlockSpec((1, d, tf), lambda j, f, te, tv: (te[j], 0, f)),
                      pl.BlockSpec((1, d, tf), lambda j, f, te, tv: (te[j], 0, nf + f)),
                      pl.BlockSpec((1, tf, d), lambda j, f, te, tv: (te[j], f, 0))],
            out_specs=pl.BlockSpec((tm, d), lambda j, f, te, tv: (j, 0)),
            scratch_shapes=[pltpu.VMEM((tm, d), BF16), pltpu.VMEM((tm, d), F32)]),
        out_shape=jax.ShapeDtypeStruct((rg, d), F32),
        compiler_params=_cparams(("arbitrary", "arbitrary")),
        name="b6_moe_ffn",
    )(te, tv, xg, w_gu, w_gu, w_d)


def _b7_body(meta_ref, meta_next_ref, route_ref, h_ref, post_ref, yg_ref, out_ref, buf_s, sem, *, tm):
    i = pl.program_id(0)
    slot = i % 2

    def gather_tile(mref, s):
        def issue(g, _):
            for u in range(SUBLANES):
                for k in range(TOP_K):
                    pos = mref[k * tm + g * SUBLANES + u]
                    pltpu.make_async_copy(_row(yg_ref, pos), buf_s.at[s, k, g, pl.ds(u, 1)], sem.at[s]).start()
            return 0

        lax.fori_loop(0, tm // SUBLANES, issue, 0)

    @pl.when(i == 0)
    def _():
        gather_tile(meta_ref, 0)

    @pl.when(i + 1 < pl.num_programs(0))
    def _():
        gather_tile(meta_next_ref, 1 - slot)

    for k in range(TOP_K):
        pltpu.make_async_copy(yg_ref.at[pl.ds(0, tm // SUBLANES)], buf_s.at[slot, k], sem.at[slot]).wait()
    route = route_ref[...]
    d = out_ref.shape[1]
    y = route[:, 0:1] * buf_s[slot, 0].reshape(tm, d) + route[:, 1:2] * buf_s[slot, 1].reshape(tm, d)
    out_ref[...] = h_ref[...] + _rms(y, post_ref[...])


def _b7(meta, route, h3, post, yg, *, tm):
    nblk = meta.shape[0] // (TOP_K * tm)
    d = h3.shape[1]
    row = lambda i: (i, 0)
    return pl.pallas_call(
        functools.partial(_b7_body, tm=tm),
        grid=(nblk,),
        in_specs=[pl.BlockSpec((TOP_K * tm,), lambda i: (i,), memory_space=pltpu.SMEM),
                  pl.BlockSpec((TOP_K * tm,), lambda i: (jnp.minimum(i + 1, nblk - 1),),
                               memory_space=pltpu.SMEM),
                  pl.BlockSpec((tm, LANES), row), pl.BlockSpec((tm, d), row),
                  pl.BlockSpec((1, d), lambda i: (0, 0)),
                  pl.BlockSpec(memory_space=pl.ANY)],
        out_specs=pl.BlockSpec((tm, d), row),
        scratch_shapes=[pltpu.VMEM((2, TOP_K, tm // SUBLANES, SUBLANES, d), F32),
                        pltpu.SemaphoreType.DMA((2,))],
        out_shape=jax.ShapeDtypeStruct((nblk * tm, d), F32),
        compiler_params=_cparams(("arbitrary",)),
        name="b7_combine",
    )(meta, meta, route, h3, post, yg)


def _pick(n, target, mult):
    best = mult
    for cand in range(mult, min(n, target) + 1, mult):
        if n % cand == 0:
            best = cand
    return best


def kernel(x, meta_tokens, a_pre_norm, a_w_in, a_conv_w, a_conv_b, a_gate_a_w, a_gate_a_b, a_gate_x_w, a_gate_x_b, a_lambda, a_w_out, a_post_norm, kv_norm, kv_w_dkv, kv_latent_norm, kv_w_ukv, b_pre_norm, b_w_dq, b_q_norm, b_w_uq, b_w_o, b_post_norm, f_pre_norm, f_w_gate_up, f_w_down, f_post_norm, m_pre_norm, m_router_w, m_router_b, m_w_gate_up, m_w_down, m_post_norm):
    bsz, seq, d = x.shape
    n_meta = meta_tokens.shape[0]
    assert a_pre_norm.shape[0] == 1 and b_pre_norm.shape[0] == 1, "one RG-LRU layer + one MLA layer"
    t_all = n_meta + seq
    assert t_all % TT == 0 and seq % ATT_BLK == 0 and bsz % 8 == 0 and n_meta % 16 == 0
    rows = t_all * bsz
    row = lambda v: v.reshape(1, -1).astype(F32)
    bf = lambda w: w.astype(BF16)

    h0 = jnp.concatenate([jnp.broadcast_to(meta_tokens.astype(F32)[:, None, :], (n_meta, bsz, d)),
                          jnp.transpose(x, (1, 0, 2))], axis=0).reshape(rows, d)

    wg = bf(jnp.concatenate([a_gate_a_w[0], a_gate_x_w[0]], axis=-1))
    h1 = _a2(h0, row(a_pre_norm[0]), bf(a_w_in[0]), a_conv_w[0].astype(F32), row(a_conv_b[0]), wg,
             row(a_gate_a_b[0]), row(a_gate_x_b[0]), row(a_lambda[0]), bf(a_w_out[0]), row(a_post_norm[0]),
             tt=TT_SMALL, bsz=bsz)
    d_ff = f_w_down.shape[1]
    h2 = _a3(h1, row(f_pre_norm[0]), bf(f_w_gate_up[0]), bf(f_w_down[0]), row(f_post_norm[0]),
             bsz=bsz, tt=TT, npad=PAD_T // TT, tf=_pick(d_ff, 1024, LANES))
    t_pad = PAD_T + t_all
    seq0 = PAD_T + n_meta
    assert seq0 % ATT_BLK == 0 and t_pad % (TT_SMALL * bsz) == 0

    kv_lora = kv_latent_norm.shape[0]
    nh = b_w_uq.shape[2] // (QK_NOPE + QK_ROPE)
    hr = QK_ROPE // 2
    scale = (QK_NOPE + QK_ROPE) ** -0.5 * math.log2(math.e)
    inv = ROPE_THETA ** (-jnp.arange(0, QK_ROPE, 2, dtype=F32) / QK_ROPE)
    pos = jnp.maximum(jnp.arange(t_pad, dtype=F32) - PAD_T, 0.0)
    ang = pos[:, None] * inv[None, :]
    cos, sin = jnp.cos(ang), jnp.sin(ang)
    one = jnp.ones((t_pad, QK_NOPE), F32)
    zero = jnp.zeros((t_pad, QK_NOPE), F32)
    tq = scale * jnp.concatenate([one, cos, cos, -sin, sin], axis=1)
    tkc = jnp.concatenate([zero, cos, cos, cos, cos], axis=1)
    tks = jnp.concatenate([zero, -sin, sin, -sin, sin], axis=1)

    w_c = kv_w_dkv[:, :kv_lora]
    k1 = kv_w_dkv[:, kv_lora:kv_lora + hr]
    k2 = kv_w_dkv[:, kv_lora + hr:]
    zk = jnp.zeros((d, QK_NOPE), F32)
    wka = jnp.concatenate([zk, k1, k2, k1, k2], axis=1)
    wkb = jnp.concatenate([zk, k2, k1, k2, k1], axis=1)
    ukv = kv_w_ukv.reshape(kv_lora, nh, QK_NOPE + V_HEAD)
    wk = jnp.concatenate([ukv[:, :, :QK_NOPE], jnp.zeros((kv_lora, nh, HEAD_PAD - QK_NOPE), F32)],
                         axis=2).reshape(kv_lora, nh * HEAD_PAD)
    wv = ukv[:, :, QK_NOPE:].reshape(kv_lora, nh * V_HEAD)
    uq = b_w_uq[0].reshape(-1, nh, QK_NOPE + QK_ROPE)
    qn_, q1, q2 = uq[:, :, :QK_NOPE], uq[:, :, QK_NOPE:QK_NOPE + hr], uq[:, :, QK_NOPE + hr:]
    wq = jnp.concatenate([qn_, q1, q2, q2, q1], axis=2).reshape(-1, nh * HEAD_PAD)
    wkv = bf(jnp.concatenate([w_c, wka, wkb], axis=1))
    q, k, v = _b1(h2.reshape(bsz * t_pad, d), row(kv_norm), row(b_pre_norm[0]), wkv,
                  row(kv_latent_norm), bf(wk), bf(wv), bf(b_w_dq[0]), row(b_q_norm[0]), bf(wq), tq, tkc, tks,
                  tm=TT_SMALL * bsz, nh=nh)

    o = _b2(q.reshape(bsz, t_pad, nh * HEAD_PAD), k.reshape(bsz, t_pad, nh * HEAD_PAD),
            v.reshape(bsz, t_pad, nh * V_HEAD), nh=nh, meta0=PAD_T, seq0=seq0, seq=seq,
            qb=_pick(seq, ATT_QBLK, ATT_BLK), nhs=ATT_HEADS)

    rows_seq = seq * bsz
    tm1 = ATT_BLK
    assert rows_seq % MOE_TILE == 0
    ne = m_router_w.shape[2]
    rw = bf(jnp.concatenate([m_router_w[0], jnp.zeros((d, LANES - ne), F32)], axis=1))
    rb = jnp.concatenate([m_router_b[0].astype(F32), jnp.full((LANES - ne,), -1e30, F32)]).reshape(1, LANES)
    h3, xn, route, cnt = _b3(o, h2, bf(b_w_o[0]), row(b_post_norm[0]), row(m_pre_norm[0]), rw, rb,
                             tm=tm1, blk0=seq0 // tm1, nper=seq // tm1)

    tmg = MOE_TILE
    meta = _b4(route, cnt, tm=tm1, tile=tmg)[:, :TOP_K, :].reshape(-1)
    counts = cnt[0, :ne].astype(I32)
    padded = ((counts + tmg - 1) // tmg) * tmg
    ends = jnp.cumsum(padded)
    nt = (TOP_K * rows_seq) // tmg + ne
    tile_start = jnp.arange(nt, dtype=I32) * tmg
    tv = (tile_start < ends[-1]).astype(I32)
    last = jnp.maximum(ends[-1] // tmg - 1, 0)
    te_raw = jnp.sum((tile_start[:, None] >= ends[None, :]).astype(I32), axis=1)
    te = jnp.minimum(jnp.where(tv > 0, te_raw, te_raw[last]), ne - 1).astype(I32)

    grp = lambda a: a.reshape(a.shape[0] // SUBLANES, SUBLANES, d)
    xg = _b5(meta, grp(xn), jnp.zeros((nt * tmg // SUBLANES, SUBLANES, d), F32), tm=tm1)
    e_ff = m_w_down.shape[2]
    yg = _b6(te, tv, xg.reshape(nt * tmg, d), bf(m_w_gate_up[0]), bf(m_w_down[0]), tm=tmg,
             tf=_pick(e_ff, MOE_FF_CHUNK, LANES))
    h4 = _b7(meta, route, h3, row(m_post_norm[0]), grp(yg), tm=tm1)
    return h4.reshape(bsz, seq, d)
```

```python
import functools
import math

import jax
import jax.numpy as jnp
from jax import lax
from jax.experimental import pallas as pl
from jax.experimental.pallas import tpu as pltpu

F32 = jnp.float32
BF16 = jnp.bfloat16
I32 = jnp.int32

RMS_EPS = 1e-6
RG_C = 8.0
ROPE_THETA = 10000.0
QK_NOPE = 64
QK_ROPE = 32
V_HEAD = 64
HEAD_PAD = 128
TOP_K = 2
LANES = 128
SUBLANES = 8

TT = 48
TT_SMALL = 24
PAD_T = 240
ATT_BLK = 256
ATT_KBLK = 256
ATT_QBLK = 2048
ATT_HEADS = 4
MOE_TILE = 512
MOE_FF_CHUNK = 1792
VMEM_LIMIT = 56 * 1024 * 1024


def _cparams(sem):
    return pltpu.CompilerParams(dimension_semantics=sem, vmem_limit_bytes=VMEM_LIMIT)


def _rms(x, g):
    ms = jnp.mean(x * x, axis=-1, keepdims=True)
    return x * lax.rsqrt(ms + RMS_EPS) * g


def _sigmoid(x):
    return 0.5 * jnp.tanh(0.5 * x) + 0.5


def _dot(a, b):
    return jnp.dot(a, b, preferred_element_type=F32)


def _dot_t(a, b):
    return lax.dot_general(a, b, (((1,), (1,)), ((), ())), preferred_element_type=F32)


def _a2_body(h_ref, hnext_ref, pre_ref, win_ref, cw_ref, cb_ref, wg_ref, gab_ref, gxb_ref, lam_ref,
             wo_ref, pn_ref, out_ref, yx_s, ext_s, a_s, b_s, carry_s, *, nb, tt, bsz, cw):
    tm = tt * bsz
    tail = (cw - 1) * bsz
    c = ext_s.shape[1]
    i = pl.program_id(0)
    slot = i % 2

    def project(href, s):
        xn = _rms(href[...], pre_ref[...]).astype(BF16)
        yx_s[s, :, 0:c] = _dot(xn, win_ref[:, 0:c]).astype(BF16)
        yx_s[s, :, c:] = _dot(xn, win_ref[:, c:]).astype(BF16)

    @pl.when(i == 0)
    def _():
        ext_s[0:tail, :] = jnp.zeros((tail, c), F32)
        carry_s[...] = jnp.zeros_like(carry_s)
        project(h_ref, 0)

    ext_s[tail:tail + tm, :] = yx_s[slot, :, c:].astype(F32)

    xn_next = _rms(hnext_ref[...], pre_ref[...]).astype(BF16)
    pw = 2 * LANES
    pieces = [(s0, min(s0 + pw, 2 * c)) for s0 in range(0, 2 * c, pw)]

    def project_piece(p):
        lo, hi = pieces[p]
        yx_s[1 - slot, :, lo:hi] = _dot(xn_next, win_ref[:, lo:hi]).astype(BF16)

    xc = cb_ref[...] + ext_s[0:tm, :] * cw_ref[0:1, :]
    for k in range(1, cw):
        xc = xc + ext_s[k * bsz:k * bsz + tm, :] * cw_ref[k:k + 1, :]
    ext_s[0:tail, :] = ext_s[tm:tm + tail, :]

    neg_lam = -lam_ref[...]
    sp = jnp.maximum(neg_lam, 0.0) + jnp.log1p(jnp.exp(-jnp.abs(neg_lam)))
    rate = sp * (-RG_C * math.log2(math.e))
    xcb = xc.astype(BF16)
    per_block = -(-len(pieces) // nb)
    for n in range(nb):
        for p in range(n * per_block, min((n + 1) * per_block, len(pieces))):
            project_piece(p)
        sl = slice(n * LANES, (n + 1) * LANES)
        g = _dot(xcb[:, sl], wg_ref[n])
        r = _sigmoid(g[:, :LANES] + gab_ref[:, sl])
        ig = _sigmoid(g[:, LANES:] + gxb_ref[:, sl])
        a = jnp.exp2(r * rate[:, sl])
        a_s[:, sl] = a
        u = 1.0 - a * a
        root = jnp.where(u > 0.0, u * lax.rsqrt(u), 0.0)
        b_s[:, sl] = root * (ig * xc[:, sl])

    def step(t, h):
        rows = pl.ds(pl.multiple_of(t * bsz, bsz), bsz)
        h = a_s[rows, :] * h + b_s[rows, :]
        b_s[rows, :] = h
        return h

    carry_s[...] = lax.fori_loop(0, tt, step, carry_s[...], unroll=4)

    z = (jax.nn.gelu(yx_s[slot, :, 0:c].astype(F32)) * b_s[...]).astype(BF16)
    mix = _dot(z, wo_ref[...])
    out_ref[...] = h_ref[...] + _rms(mix, pn_ref[...])


def _a2(h0, pre, w_in, conv_w, conv_b, wg, ga_b, gx_b, lam, w_out, post, *, tt, bsz):
    r, d = h0.shape
    c = w_in.shape[1] // 2
    nb = wg.shape[0]
    cw = conv_w.shape[0]
    tm = tt * bsz
    nblk = r // tm
    row = lambda i: (i, 0)
    fixed2 = lambda i: (0, 0)
    fixed3 = lambda i: (0, 0, 0)
    return pl.pallas_call(
        functools.partial(_a2_body, nb=nb, tt=tt, bsz=bsz, cw=cw),
        grid=(nblk,),
        in_specs=[pl.BlockSpec((tm, d), row),
                  pl.BlockSpec((tm, d), lambda i: (jnp.minimum(i + 1, nblk - 1), 0)),
                  pl.BlockSpec((1, d), fixed2),
                  pl.BlockSpec((d, 2 * c), fixed2, pipeline_mode=pl.Buffered(1)),
                  pl.BlockSpec((cw, c), fixed2), pl.BlockSpec((1, c), fixed2),
                  pl.BlockSpec((nb, LANES, 2 * LANES), fixed3),
                  pl.BlockSpec((1, c), fixed2), pl.BlockSpec((1, c), fixed2), pl.BlockSpec((1, c), fixed2),
                  pl.BlockSpec((c, d), fixed2, pipeline_mode=pl.Buffered(1)), pl.BlockSpec((1, d), fixed2)],
        out_specs=pl.BlockSpec((tm, d), row),
        out_shape=jax.ShapeDtypeStruct((r, d), F32),
        scratch_shapes=[pltpu.VMEM((2, tm, 2 * c), BF16),
                        pltpu.VMEM((tm + (cw - 1) * bsz, c), F32), pltpu.VMEM((tm, c), F32),
                        pltpu.VMEM((tm, c), F32), pltpu.VMEM((bsz, c), F32)],
        compiler_params=_cparams(("arbitrary",)),
        name="a2_rglru",
    )(h0, h0, pre, w_in, conv_w, conv_b, wg, ga_b, gx_b, lam, w_out, post)


def _a3_body(h_ref, pre_ref, wgu_ref, wd_ref, post_ref, out_ref, tr_s, *, npad, bsz, tt, tf):
    i = pl.program_id(0)

    @pl.when(i < npad)
    def _():
        out_ref[...] = jnp.zeros_like(out_ref)

    @pl.when(i >= npad)
    def _():
        h = h_ref[...]
        xn = _rms(h, pre_ref[...]).astype(BF16)
        ff = wd_ref.shape[0]
        acc = None
        for c in range(ff // tf):
            g = _dot(xn, wgu_ref[:, c * tf:(c + 1) * tf])
            u = _dot(xn, wgu_ref[:, ff + c * tf:ff + (c + 1) * tf])
            part = _dot((g * _sigmoid(g) * u).astype(BF16), wd_ref[c * tf:(c + 1) * tf, :])
            acc = part if acc is None else acc + part
        res = h + _rms(acc, post_ref[...])
        for c in range(tr_s.shape[0]):
            tr_s[c] = res[:, c * LANES:(c + 1) * LANES]
        for b in range(bsz):
            for c in range(tr_s.shape[0]):
                out_ref[b, :, c * LANES:(c + 1) * LANES] = tr_s[c, pl.ds(b, tt, stride=bsz), :]


def _a3(h1, pre, w_gu, w_d, post, *, bsz, tt, npad, tf):
    r, d = h1.shape
    tm = tt * bsz
    nblk = r // tm
    fixed = lambda i: (0, 0)
    resident = lambda a: pl.BlockSpec(a.shape, fixed, pipeline_mode=pl.Buffered(1))
    return pl.pallas_call(
        functools.partial(_a3_body, npad=npad, bsz=bsz, tt=tt, tf=tf),
        grid=(npad + nblk,),
        in_specs=[pl.BlockSpec((tm, d), lambda i: (jnp.maximum(i - npad, 0), 0)), pl.BlockSpec((1, d), fixed),
                  resident(w_gu), resident(w_d), pl.BlockSpec((1, d), fixed)],
        out_specs=pl.BlockSpec((bsz, tt, d), lambda i: (0, i, 0)),
        out_shape=jax.ShapeDtypeStruct((bsz, (npad + nblk) * tt, d), F32),
        scratch_shapes=[pltpu.VMEM((d // LANES, tm, LANES), F32)],
        compiler_params=_cparams(("arbitrary",)),
        name="a3_dense_ffn",
    )(h1, pre, w_gu, w_d, post)


def _b1_body(h_ref, kvn_ref, bpre_ref, wkv_ref, latn_ref, wk_ref, wv_ref,
             wdq_ref, qn_ref, wq_ref, tq_ref, tkc_ref, tks_ref, q_ref, k_ref, v_ref, *, nh):
    h = h_ref[...]
    hn = h * lax.rsqrt(jnp.mean(h * h, axis=-1, keepdims=True) + RMS_EPS)
    ukv = (hn * kvn_ref[...]).astype(BF16)
    uq = (hn * bpre_ref[...]).astype(BF16)

    lat = wkv_ref.shape[1] - 2 * HEAD_PAD
    dkv = _dot(ukv, wkv_ref[...])
    ckv = _rms(dkv[:, :lat], latn_ref[...]).astype(BF16)
    kr = dkv[:, lat:lat + HEAD_PAD] * tkc_ref[...] + dkv[:, lat + HEAD_PAD:] * tks_ref[...]
    kn = _dot(ckv, wk_ref[...])
    for hd in range(nh):
        sl = slice(hd * HEAD_PAD, (hd + 1) * HEAD_PAD)
        k_ref[:, sl] = (kn[:, sl] + kr).astype(BF16)
    v_ref[...] = _dot(ckv, wv_ref[...]).astype(BF16)

    cq = _rms(_dot(uq, wdq_ref[...]), qn_ref[...]).astype(BF16)
    qr = _dot(cq, wq_ref[...])
    tq = tq_ref[...]
    for hd in range(nh):
        sl = slice(hd * HEAD_PAD, (hd + 1) * HEAD_PAD)
        q_ref[:, sl] = (qr[:, sl] * tq).astype(BF16)


def _b1(h2, kvn, bpre, wkv, latn, wk, wv, wdq, qn, wq, tq, tkc, tks, *, tm, nh):
    r, d = h2.shape
    nper = tq.shape[0] // tm
    row = lambda i: (i, 0)
    tab = lambda i: (i % nper, 0)
    fixed = lambda i: (0, 0)
    full = lambda a: pl.BlockSpec(a.shape, fixed)
    return pl.pallas_call(
        functools.partial(_b1_body, nh=nh),
        grid=(r // tm,),
        in_specs=[pl.BlockSpec((tm, d), row), full(kvn), full(bpre), full(wkv),
                  full(latn), full(wk), full(wv), full(wdq), full(qn), full(wq),
                  pl.BlockSpec((tm, HEAD_PAD), tab), pl.BlockSpec((tm, HEAD_PAD), tab),
                  pl.BlockSpec((tm, HEAD_PAD), tab)],
        out_specs=[pl.BlockSpec((tm, nh * HEAD_PAD), row), pl.BlockSpec((tm, nh * HEAD_PAD), row),
                   pl.BlockSpec((tm, nh * V_HEAD), row)],
        out_shape=[jax.ShapeDtypeStruct((r, nh * HEAD_PAD), BF16),
                   jax.ShapeDtypeStruct((r, nh * HEAD_PAD), BF16),
                   jax.ShapeDtypeStruct((r, nh * V_HEAD), BF16)],
        compiler_params=_cparams(("arbitrary",)),
        name="b1_qkv_proj",
    )(h2, kvn, bpre, wkv, latn, wk, wv, wdq, qn, wq, tq, tkc, tks)


def _tile2(x):
    return jnp.concatenate([x, x], axis=1)


def _b2_body(q_ref, k_ref, v_ref, o_ref, *scratch, meta0, seq0, nq, qb, kb, nhs):
    pair = 2 * V_HEAD
    t_pad = v_ref.shape[1]
    nsub = qb // kb
    npairs = nhs // 2
    vext_s = scratch[0:npairs]
    m_s = scratch[npairs:npairs + nhs]
    acc_s = scratch[npairs + nhs:npairs + 2 * nhs]
    for pp in range(npairs):
        vext_s[pp][meta0:, 0:pair] = v_ref[0, meta0:, pp * pair:(pp + 1) * pair]
        vext_s[pp][meta0:, pair:] = jnp.ones((t_pad - meta0, pair), BF16)
    o_ref[0, 0:seq0, :] = jnp.zeros((seq0, npairs * pair), o_ref.dtype)
    lane = lax.broadcasted_iota(I32, (qb, pair), 1)
    neg = jnp.finfo(F32).min
    heads = range(nhs)

    def scores(hh, qh, qlo, krows):
        return _dot_t(qh[qlo:], k_ref[0, krows, hh * HEAD_PAD:(hh + 1) * HEAD_PAD])

    def absorb(hh, s, qlo, krows, masked):
        if masked:
            rowi = lax.broadcasted_iota(I32, s.shape, 0)
            coli = lax.broadcasted_iota(I32, s.shape, 1)
            s = jnp.where(coli <= rowi, s, neg)
        m_old = m_s[hh][qlo:, :]
        m_new = jnp.maximum(m_old, jnp.max(s, axis=-1, keepdims=True))
        alpha = jnp.exp2(m_old - m_new)
        p = jnp.exp2(s - jnp.concatenate([m_new] * (s.shape[1] // pair), axis=1))
        acc_s[hh][qlo:, :] = (_tile2(alpha) * acc_s[hh][qlo:, :]
                              + _dot(p.astype(BF16), vext_s[hh // 2][krows, :]))
        m_s[hh][qlo:, :] = m_new

    def qblock(i, _):
        q0 = pl.multiple_of(seq0 + i * qb, ATT_BLK)
        qs = [q_ref[0, pl.ds(q0, qb), hh * HEAD_PAD:(hh + 1) * HEAD_PAD] for hh in heads]
        for hh in heads:
            s = _dot_t(qs[hh], k_ref[0, meta0:seq0, hh * HEAD_PAD:(hh + 1) * HEAD_PAD])
            m = jnp.max(s, axis=-1, keepdims=True)
            p = jnp.exp2(s - m)
            acc_s[hh][...] = _dot(p.astype(BF16), vext_s[hh // 2][meta0:seq0, :])
            m_s[hh][...] = jnp.broadcast_to(m, (qb, pair))

        def kv_step(j, _):
            krows = pl.ds(pl.multiple_of(seq0 + j * kb, ATT_BLK), kb)
            for hh in heads:
                absorb(hh, scores(hh, qs[hh], 0, krows), 0, krows, False)
            return 0

        lax.fori_loop(0, i * nsub, kv_step, 0)
        for jj in range(nsub):
            krows = pl.ds(pl.multiple_of(q0 + jj * kb, ATT_BLK), kb)
            for hh in heads:
                absorb(hh, scores(hh, qs[hh], jj * kb, krows), jj * kb, krows, True)
        for pp in range(npairs):
            res = []
            for hh in (2 * pp, 2 * pp + 1):
                acc = acc_s[hh][...]
                res.append(acc[:, :pair] / acc[:, pair:])
            o_ref[0, pl.ds(q0, qb), pp * pair:(pp + 1) * pair] = jnp.where(
                lane < V_HEAD, res[0], res[1]).astype(o_ref.dtype)
        return 0

    lax.fori_loop(0, nq, qblock, 0)


def _b2(q, k, v, *, nh, meta0, seq0, seq, qb, nhs):
    bsz, t_pad, _ = q.shape
    qk = pl.BlockSpec((1, t_pad, nhs * HEAD_PAD), lambda b, p: (b, 0, p))
    vo = pl.BlockSpec((1, t_pad, nhs * V_HEAD), lambda b, p: (b, 0, p))
    return pl.pallas_call(
        functools.partial(_b2_body, meta0=meta0, seq0=seq0, nq=seq // qb, qb=qb, kb=min(ATT_KBLK, qb), nhs=nhs),
        grid=(bsz, nh // nhs),
        in_specs=[qk, qk, vo],
        out_specs=vo,
        out_shape=jax.ShapeDtypeStruct(v.shape, BF16),
        scratch_shapes=([pltpu.VMEM((t_pad, 4 * V_HEAD), BF16)] * (nhs // 2)
                        + [pltpu.VMEM((qb, 2 * V_HEAD), F32)] * nhs
                        + [pltpu.VMEM((qb, 4 * V_HEAD), F32)] * nhs),
        compiler_params=_cparams(("arbitrary", "arbitrary")),
        name="b2_attention",
    )(q, k, v)


B3_TILES = 4


def _b3_body(*refs, tm):
    o_refs = refs[0:2 * B3_TILES:2]
    h_refs = refs[1:2 * B3_TILES:2]
    wo_ref, post_ref, mpre_ref, rw_ref, rb_ref, h3_ref, xn_ref, route_ref, cnt_ref, cnt_s = refs[2 * B3_TILES:]
    tiles = range(B3_TILES)

    @pl.when(pl.program_id(0) == 0)
    def _():
        cnt_s[...] = jnp.zeros_like(cnt_s)

    mix = [_dot(o_refs[t][0], wo_ref[...]) for t in tiles]
    h3 = [h_refs[t][0] + _rms(mix[t], post_ref[...]) for t in tiles]
    xn = [_rms(h3[t], mpre_ref[...]) for t in tiles]
    logits = [_dot(xn[t].astype(BF16), rw_ref[...]) + rb_ref[...] for t in tiles]
    lane = lax.broadcasted_iota(I32, logits[0].shape, 1).astype(F32)
    big = float(LANES)
    count = cnt_s[...]
    for t in tiles:
        rows = slice(t * tm, (t + 1) * tm)
        h3_ref[rows, :] = h3[t]
        xn_ref[rows, :] = xn[t]
        m1 = jnp.max(logits[t], axis=-1, keepdims=True)
        i1 = jnp.min(jnp.where(logits[t] == m1, lane, big), axis=-1, keepdims=True)
        rest = jnp.where(lane == i1, -jnp.inf, logits[t])
        m2 = jnp.max(rest, axis=-1, keepdims=True)
        i2 = jnp.min(jnp.where(rest == m2, lane, big), axis=-1, keepdims=True)
        e2 = jnp.exp(m2 - m1)
        g1 = 1.0 / (1.0 + e2)
        g2 = e2 / (1.0 + e2)
        route_ref[rows, :] = jnp.where(lane == 0.0, g1, jnp.where(lane == 1.0, g2,
                                       jnp.where(lane == 2.0, i1, jnp.where(lane == 3.0, i2, 0.0))))
        onehot = jnp.where((lane == i1) | (lane == i2), 1.0, 0.0)
        count = count + jnp.sum(onehot, axis=0, keepdims=True)
    cnt_s[...] = count
    cnt_ref[...] = jnp.broadcast_to(count, cnt_ref.shape)


def _b3(o, h2, w_o, post, mpre, rw, rb, *, tm, blk0, nper):
    bsz, _, d = h2.shape
    nblk = bsz * nper
    assert nblk % B3_TILES == 0
    src = lambda t: (lambda i: ((B3_TILES * i + t) // nper, blk0 + (B3_TILES * i + t) % nper, 0))
    dst = lambda i: (i, 0)
    fixed = lambda i: (0, 0)
    rows = nblk * tm
    step = B3_TILES * tm
    tile_specs = []
    for t in range(B3_TILES):
        tile_specs += [pl.BlockSpec((1, tm, o.shape[2]), src(t)), pl.BlockSpec((1, tm, d), src(t))]
    return pl.pallas_call(
        functools.partial(_b3_body, tm=tm),
        grid=(nblk // B3_TILES,),
        in_specs=tile_specs + [pl.BlockSpec(w_o.shape, fixed), pl.BlockSpec((1, d), fixed),
                               pl.BlockSpec((1, d), fixed), pl.BlockSpec(rw.shape, fixed),
                               pl.BlockSpec((1, LANES), fixed)],
        out_specs=[pl.BlockSpec((step, d), dst), pl.BlockSpec((step, d), dst), pl.BlockSpec((step, LANES), dst),
                   pl.BlockSpec((8, LANES), fixed)],
        out_shape=[jax.ShapeDtypeStruct((rows, d), F32), jax.ShapeDtypeStruct((rows, d), F32),
                   jax.ShapeDtypeStruct((rows, LANES), F32), jax.ShapeDtypeStruct((8, LANES), F32)],
        scratch_shapes=[pltpu.VMEM((1, LANES), F32)],
        compiler_params=_cparams(("arbitrary",)),
        name="b3_oproj_router",
    )(*([o, h2] * B3_TILES), w_o, post, mpre, rw, rb)


B4_TILES = 4


def _b4_body(route_ref, cnt_ref, meta_ref, run_s, off_s, *, tile, tm):
    lane = lax.broadcasted_iota(I32, (tm, LANES), 1).astype(F32)

    @pl.when(pl.program_id(0) == 0)
    def _():
        total = cnt_ref[0:1, :]
        ntiles = jnp.floor((total + (tile - 1)) * (1.0 / tile))
        rr = lax.broadcasted_iota(I32, (LANES, LANES), 0)
        cc = lax.broadcasted_iota(I32, (LANES, LANES), 1)
        upper = jnp.where(rr < cc, 1.0, 0.0).astype(BF16)
        before = _dot(jnp.broadcast_to(ntiles, (8, LANES)).astype(BF16), upper)
        off_s[...] = before[0:1, :] * float(tile)
        run_s[...] = jnp.zeros_like(run_s)

    rr = lax.broadcasted_iota(I32, (tm, tm), 0)
    cc = lax.broadcasted_iota(I32, (tm, tm), 1)
    lower = jnp.where(cc < rr, 1.0, 0.0).astype(BF16)
    base = run_s[...] + off_s[...]
    for t in range(B4_TILES):
        route = route_ref[t * tm:(t + 1) * tm, :]
        i1 = jnp.sum(jnp.where(lane == 2.0, route, 0.0), axis=-1, keepdims=True)
        i2 = jnp.sum(jnp.where(lane == 3.0, route, 0.0), axis=-1, keepdims=True)
        oh1 = lane == i1
        oh2 = lane == i2
        onehot = jnp.where(oh1 | oh2, 1.0, 0.0)
        pos = _dot(lower, onehot.astype(BF16)) + base
        p1 = jnp.sum(jnp.where(oh1, pos, 0.0), axis=-1, keepdims=True)
        p2 = jnp.sum(jnp.where(oh2, pos, 0.0), axis=-1, keepdims=True)
        packed = jnp.where(lane == 0.0, p1, jnp.where(lane == 1.0, p2, 0.0))
        meta_ref[t] = jnp.transpose(packed)[0:8, :].astype(I32)
        base = base + jnp.sum(onehot, axis=0, keepdims=True)
    run_s[...] = base - off_s[...]


def _b4(route, cnt, *, tm, tile):
    nblk = route.shape[0] // tm
    assert nblk % B4_TILES == 0
    return pl.pallas_call(
        functools.partial(_b4_body, tile=tile, tm=tm),
        grid=(nblk // B4_TILES,),
        in_specs=[pl.BlockSpec((B4_TILES * tm, LANES), lambda i: (i, 0)),
                  pl.BlockSpec((8, LANES), lambda i: (0, 0))],
        out_specs=pl.BlockSpec((B4_TILES, 8, tm), lambda i: (i, 0, 0)),
        out_shape=jax.ShapeDtypeStruct((nblk, 8, tm), I32),
        scratch_shapes=[pltpu.VMEM((1, LANES), F32), pltpu.VMEM((1, LANES), F32)],
        compiler_params=_cparams(("arbitrary",)),
        name="b4_route_rank",
    )(route, cnt)


def _row(ref, pos):
    return ref.at[pos >> 3, pl.ds(pos & (SUBLANES - 1), 1)]


DISPATCH_BUFS = 3


def _b5_body(meta_ref, xn_ref, xg_in_ref, xg_ref, buf_s, in_sem, out_sem, *, tm):
    del xg_in_ref
    i = pl.program_id(0)
    n = pl.num_programs(0)
    groups = tm // SUBLANES
    slot = i % DISPATCH_BUFS
    nxt = (i + 1) % DISPATCH_BUFS

    def fetch(t, s):
        return pltpu.make_async_copy(xn_ref.at[pl.ds(t * groups, groups)], buf_s.at[s], in_sem.at[s])

    def drain(s):
        for k in range(TOP_K):
            pltpu.make_async_copy(buf_s.at[s], xg_ref.at[pl.ds(0, groups)], out_sem.at[s]).wait()

    @pl.when(i == 0)
    def _():
        fetch(0, 0).start()

    @pl.when(i >= DISPATCH_BUFS - 1)
    def _():
        drain(nxt)

    @pl.when(i + 1 < n)
    def _():
        fetch(i + 1, nxt).start()

    fetch(i, slot).wait()

    def issue(g, _):
        for u in range(SUBLANES):
            for k in range(TOP_K):
                pos = meta_ref[k * tm + g * SUBLANES + u]
                pltpu.make_async_copy(buf_s.at[slot, g, pl.ds(u, 1)], _row(xg_ref, pos), out_sem.at[slot]).start()
        return 0

    lax.fori_loop(0, groups, issue, 0)

    @pl.when(i == n - 1)
    def _():
        for back in range(DISPATCH_BUFS - 1):
            @pl.when(i >= back)
            def _():
                drain((i - back) % DISPATCH_BUFS)


def _b5(meta, xn, xg0, *, tm):
    nblk = meta.shape[0] // (TOP_K * tm)
    return pl.pallas_call(
        functools.partial(_b5_body, tm=tm),
        grid=(nblk,),
        in_specs=[pl.BlockSpec((TOP_K * tm,), lambda i: (i,), memory_space=pltpu.SMEM),
                  pl.BlockSpec(memory_space=pl.ANY),
                  pl.BlockSpec(memory_space=pl.ANY)],
        out_specs=pl.BlockSpec(memory_space=pl.ANY),
        scratch_shapes=[pltpu.VMEM((DISPATCH_BUFS, tm // SUBLANES) + xn.shape[1:], xn.dtype),
                        pltpu.SemaphoreType.DMA((DISPATCH_BUFS,)),
                        pltpu.SemaphoreType.DMA((DISPATCH_BUFS,))],
        out_shape=jax.ShapeDtypeStruct(xg0.shape, xg0.dtype),
        input_output_aliases={2: 0},
        compiler_params=_cparams(("arbitrary",)),
        name="b5_dispatch",
    )(meta, xn, xg0)


def _b6_body(te_ref, tv_ref, x_ref, wg_ref, wu_ref, wd_ref, y_ref, xb_s, acc_s):
    j = pl.program_id(0)
    f = pl.program_id(1)

    @pl.when(f == 0)
    def _():
        xb_s[...] = x_ref[...].astype(BF16)
        acc_s[...] = jnp.zeros_like(acc_s)

    @pl.when(tv_ref[j] > 0)
    def _():
        xb = xb_s[...]
        g = _dot(xb, wg_ref[0])
        u = _dot(xb, wu_ref[0])
        acc_s[...] += _dot((g * _sigmoid(g) * u).astype(BF16), wd_ref[0])

    @pl.when(f == pl.num_programs(1) - 1)
    def _():
        y_ref[...] = acc_s[...]


def _b6(te, tv, xg, w_gu, w_d, *, tm, tf):
    rg, d = xg.shape
    eff = w_d.shape[1]
    nf = eff // tf
    nt = rg // tm
    return pl.pallas_call(
        _b6_body,
        grid_spec=pltpu.PrefetchScalarGridSpec(
            num_scalar_prefetch=2,
            grid=(nt, nf),
            in_specs=[pl.BlockSpec((tm, d), lambda j, f, te, tv: (j, 0)),
                      pl.BlockSpec((1, d, tf), lambda j, f, te, tv: (te[j], 0, f)),
                      pl.BlockSpec((1, d, tf), lambda j, f, te, tv: (te[j], 0, nf + f)),
                      pl.BlockSpec((1, tf, d), lambda j, f, te, tv: (te[j], f, 0))],
            out_specs=pl.BlockSpec((tm, d), lambda j, f, te, tv: (j, 0)),
            scratch_shapes=[pltpu.VMEM((tm, d), BF16), pltpu.VMEM((tm, d), F32)]),
        out_shape=jax.ShapeDtypeStruct((rg, d), F32),
        compiler_params=_cparams(("arbitrary", "arbitrary")),
        name="b6_moe_ffn",
    )(te, tv, xg, w_gu, w_gu, w_d)


def _b7_body(meta_ref, meta_next_ref, route_ref, h_ref, post_ref, yg_ref, out_ref, buf_s, sem, *, tm):
    i = pl.program_id(0)
    slot = i % 2

    def gather_tile(mref, s):
        def issue(g, _):
            for u in range(SUBLANES):
                for k in range(TOP_K):
                    pos = mref[k * tm + g * SUBLANES + u]
                    pltpu.make_async_copy(_row(yg_ref, pos), buf_s.at[s, k, g, pl.ds(u, 1)], sem.at[s]).start()
            return 0

        lax.fori_loop(0, tm // SUBLANES, issue, 0)

    @pl.when(i == 0)
    def _():
        gather_tile(meta_ref, 0)

    @pl.when(i + 1 < pl.num_programs(0))
    def _():
        gather_tile(meta_next_ref, 1 - slot)

    for k in range(TOP_K):
        pltpu.make_async_copy(yg_ref.at[pl.ds(0, tm // SUBLANES)], buf_s.at[slot, k], sem.at[slot]).wait()
    route = route_ref[...]
    d = out_ref.shape[1]
    y = route[:, 0:1] * buf_s[slot, 0].reshape(tm, d) + route[:, 1:2] * buf_s[slot, 1].reshape(tm, d)
    out_ref[...] = h_ref[...] + _rms(y, post_ref[...])


def _b7(meta, route, h3, post, yg, *, tm):
    nblk = meta.shape[0] // (TOP_K * tm)
    d = h3.shape[1]
    row = lambda i: (i, 0)
    return pl.pallas_call(
        functools.partial(_b7_body, tm=tm),
        grid=(nblk,),
        in_specs=[pl.BlockSpec((TOP_K * tm,), lambda i: (i,), memory_space=pltpu.SMEM),
                  pl.BlockSpec((TOP_K * tm,), lambda i: (jnp.minimum(i + 1, nblk - 1),),
                               memory_space=pltpu.SMEM),
                  pl.BlockSpec((tm, LANES), row), pl.BlockSpec((tm, d), row),
                  pl.BlockSpec((1, d), lambda i: (0, 0)),
                  pl.BlockSpec(memory_space=pl.ANY)],
        out_specs=pl.BlockSpec((tm, d), row),
        scratch_shapes=[pltpu.VMEM((2, TOP_K, tm // SUBLANES, SUBLANES, d), F32),
                        pltpu.SemaphoreType.DMA((2,))],
        out_shape=jax.ShapeDtypeStruct((nblk * tm, d), F32),
        compiler_params=_cparams(("arbitrary",)),
        name="b7_combine",
    )(meta, meta, route, h3, post, yg)


def _pick(n, target, mult):
    best = mult
    for cand in range(mult, min(n, target) + 1, mult):
        if n % cand == 0:
            best = cand
    return best


def kernel(x, meta_tokens, a_pre_norm, a_w_in, a_conv_w, a_conv_b, a_gate_a_w, a_gate_a_b, a_gate_x_w, a_gate_x_b, a_lambda, a_w_out, a_post_norm, kv_norm, kv_w_dkv, kv_latent_norm, kv_w_ukv, b_pre_norm, b_w_dq, b_q_norm, b_w_uq, b_w_o, b_post_norm, f_pre_norm, f_w_gate_up, f_w_down, f_post_norm, m_pre_norm, m_router_w, m_router_b, m_w_gate_up, m_w_down, m_post_norm):
    bsz, seq, d = x.shape
    n_meta = meta_tokens.shape[0]
    assert a_pre_norm.shape[0] == 1 and b_pre_norm.shape[0] == 1, "one RG-LRU layer + one MLA layer"
    t_all = n_meta + seq
    assert t_all % TT == 0 and seq % ATT_BLK == 0 and bsz % 8 == 0 and n_meta % 16 == 0
    rows = t_all * bsz
    row = lambda v: v.reshape(1, -1).astype(F32)
    bf = lambda w: w.astype(BF16)

    h0 = jnp.concatenate([jnp.broadcast_to(meta_tokens.astype(F32)[:, None, :], (n_meta, bsz, d)),
                          jnp.transpose(x, (1, 0, 2))], axis=0).reshape(rows, d)

    wg = bf(jnp.concatenate([a_gate_a_w[0], a_gate_x_w[0]], axis=-1))
    h1 = _a2(h0, row(a_pre_norm[0]), bf(a_w_in[0]), a_conv_w[0].astype(F32), row(a_conv_b[0]), wg,
             row(a_gate_a_b[0]), row(a_gate_x_b[0]), row(a_lambda[0]), bf(a_w_out[0]), row(a_post_norm[0]),
             tt=TT_SMALL, bsz=bsz)
    d_ff = f_w_down.shape[1]
    h2 = _a3(h1, row(f_pre_norm[0]), bf(f_w_gate_up[0]), bf(f_w_down[0]), row(f_post_norm[0]),
             bsz=bsz, tt=TT, npad=PAD_T // TT, tf=_pick(d_ff, 1024, LANES))
    t_pad = PAD_T + t_all
    seq0 = PAD_T + n_meta
    assert seq0 % ATT_BLK == 0 and t_pad % (TT_SMALL * bsz) == 0

    kv_lora = kv_latent_norm.shape[0]
    nh = b_w_uq.shape[2] // (QK_NOPE + QK_ROPE)
    hr = QK_ROPE // 2
    scale = (QK_NOPE + QK_ROPE) ** -0.5 * math.log2(math.e)
    inv = ROPE_THETA ** (-jnp.arange(0, QK_ROPE, 2, dtype=F32) / QK_ROPE)
    pos = jnp.maximum(jnp.arange(t_pad, dtype=F32) - PAD_T, 0.0)
    ang = pos[:, None] * inv[None, :]
    cos, sin = jnp.cos(ang), jnp.sin(ang)
    one = jnp.ones((t_pad, QK_NOPE), F32)
    zero = jnp.zeros((t_pad, QK_NOPE), F32)
    tq = scale * jnp.concatenate([one, cos, cos, -sin, sin], axis=1)
    tkc = jnp.concatenate([zero, cos, cos, cos, cos], axis=1)
    tks = jnp.concatenate([zero, -sin, sin, -sin, sin], axis=1)

    w_c = kv_w_dkv[:, :kv_lora]
    k1 = kv_w_dkv[:, kv_lora:kv_lora + hr]
    k2 = kv_w_dkv[:, kv_lora + hr:]
    zk = jnp.zeros((d, QK_NOPE), F32)
    wka = jnp.concatenate([zk, k1, k2, k1, k2], axis=1)
    wkb = jnp.concatenate([zk, k2, k1, k2, k1], axis=1)
    ukv = kv_w_ukv.reshape(kv_lora, nh, QK_NOPE + V_HEAD)
    wk = jnp.concatenate([ukv[:, :, :QK_NOPE], jnp.zeros((kv_lora, nh, HEAD_PAD - QK_NOPE), F32)],
                         axis=2).reshape(kv_lora, nh * HEAD_PAD)
    wv = ukv[:, :, QK_NOPE:].reshape(kv_lora, nh * V_HEAD)
    uq = b_w_uq[0].reshape(-1, nh, QK_NOPE + QK_ROPE)
    qn_, q1, q2 = uq[:, :, :QK_NOPE], uq[:, :, QK_NOPE:QK_NOPE + hr], uq[:, :, QK_NOPE + hr:]
    wq = jnp.concatenate([qn_, q1, q2, q2, q1], axis=2).reshape(-1, nh * HEAD_PAD)
    wkv = bf(jnp.concatenate([w_c, wka, wkb], axis=1))
    q, k, v = _b1(h2.reshape(bsz * t_pad, d), row(kv_norm), row(b_pre_norm[0]), wkv,
                  row(kv_latent_norm), bf(wk), bf(wv), bf(b_w_dq[0]), row(b_q_norm[0]), bf(wq), tq, tkc, tks,
                  tm=TT_SMALL * bsz, nh=nh)

    o = _b2(q.reshape(bsz, t_pad, nh * HEAD_PAD), k.reshape(bsz, t_pad, nh * HEAD_PAD),
            v.reshape(bsz, t_pad, nh * V_HEAD), nh=nh, meta0=PAD_T, seq0=seq0, seq=seq,
            qb=_pick(seq, ATT_QBLK, ATT_BLK), nhs=ATT_HEADS)

    rows_seq = seq * bsz
    tm1 = ATT_BLK
    assert rows_seq % MOE_TILE == 0
    ne = m_router_w.shape[2]
    rw = bf(jnp.concatenate([m_router_w[0], jnp.zeros((d, LANES - ne), F32)], axis=1))
    rb = jnp.concatenate([m_router_b[0].astype(F32), jnp.full((LANES - ne,), -1e30, F32)]).reshape(1, LANES)
    h3, xn, route, cnt = _b3(o, h2, bf(b_w_o[0]), row(b_post_norm[0]), row(m_pre_norm[0]), rw, rb,
                             tm=tm1, blk0=seq0 // tm1, nper=seq // tm1)

    tmg = MOE_TILE
    meta = _b4(route, cnt, tm=tm1, tile=tmg)[:, :TOP_K, :].reshape(-1)
    counts = cnt[0, :ne].astype(I32)
    padded = ((counts + tmg - 1) // tmg) * tmg
    ends = jnp.cumsum(padded)
    nt = (TOP_K * rows_seq) // tmg + ne
    tile_start = jnp.arange(nt, dtype=I32) * tmg
    tv = (tile_start < ends[-1]).astype(I32)
    last = jnp.maximum(ends[-1] // tmg - 1, 0)
    te_raw = jnp.sum((tile_start[:, None] >= ends[None, :]).astype(I32), axis=1)
    te = jnp.minimum(jnp.where(tv > 0, te_raw, te_raw[last]), ne - 1).astype(I32)

    grp = lambda a: a.reshape(a.shape[0] // SUBLANES, SUBLANES, d)
    xg = _b5(meta, grp(xn), jnp.zeros((nt * tmg // SUBLANES, SUBLANES, d), F32), tm=tm1)
    e_ff = m_w_down.shape[2]
    yg = _b6(te, tv, xg.reshape(nt * tmg, d), bf(m_w_gate_up[0]), bf(m_w_down[0]), tm=tmg,
             tf=_pick(e_ff, MOE_FF_CHUNK, LANES))
    h4 = _b7(meta, route, h3, row(m_post_norm[0]), grp(yg), tm=tm1)
    return h4.reshape(bsz, seq, d)
```

```python
import functools
import math

import jax
import jax.numpy as jnp
from jax import lax
from jax.experimental import pallas as pl
from jax.experimental.pallas import tpu as pltpu

F32 = jnp.float32
BF16 = jnp.bfloat16
I32 = jnp.int32

RMS_EPS = 1e-6
RG_C = 8.0
ROPE_THETA = 10000.0
QK_NOPE = 64
QK_ROPE = 32
V_HEAD = 64
HEAD_PAD = 128
TOP_K = 2
LANES = 128
SUBLANES = 8

TT = 48
TT_SMALL = 24
PAD_T = 240
ATT_BLK = 256
ATT_KBLK = 256
ATT_QBLK = 2048
ATT_HEADS = 4
MOE_TILE = 512
MOE_FF_CHUNK = 1792
VMEM_LIMIT = 56 * 1024 * 1024


def _cparams(sem):
    return pltpu.CompilerParams(dimension_semantics=sem, vmem_limit_bytes=VMEM_LIMIT)


def _rms(x, g):
    ms = jnp.mean(x * x, axis=-1, keepdims=True)
    return x * lax.rsqrt(ms + RMS_EPS) * g


def _sigmoid(x):
    return 0.5 * jnp.tanh(0.5 * x) + 0.5


def _dot(a, b):
    return jnp.dot(a, b, preferred_element_type=F32)


def _dot_t(a, b):
    return lax.dot_general(a, b, (((1,), (1,)), ((), ())), preferred_element_type=F32)


def _a2_body(h_ref, hnext_ref, pre_ref, win_ref, cw_ref, cb_ref, wg_ref, gab_ref, gxb_ref, lam_ref,
             wo_ref, pn_ref, out_ref, yx_s, ext_s, a_s, b_s, carry_s, *, nb, tt, bsz, cw):
    tm = tt * bsz
    tail = (cw - 1) * bsz
    c = ext_s.shape[1]
    i = pl.program_id(0)
    slot = i % 2

    def project(href, s):
        xn = _rms(href[...], pre_ref[...]).astype(BF16)
        yx_s[s, :, 0:c] = _dot(xn, win_ref[:, 0:c]).astype(BF16)
        yx_s[s, :, c:] = _dot(xn, win_ref[:, c:]).astype(BF16)

    @pl.when(i == 0)
    def _():
        ext_s[0:tail, :] = jnp.zeros((tail, c), F32)
        carry_s[...] = jnp.zeros_like(carry_s)
        project(h_ref, 0)

    ext_s[tail:tail + tm, :] = yx_s[slot, :, c:].astype(F32)

    xn_next = _rms(hnext_ref[...], pre_ref[...]).astype(BF16)
    pw = 2 * LANES
    pieces = [(s0, min(s0 + pw, 2 * c)) for s0 in range(0, 2 * c, pw)]

    def project_piece(p):
        lo, hi = pieces[p]
        yx_s[1 - slot, :, lo:hi] = _dot(xn_next, win_ref[:, lo:hi]).astype(BF16)

    xc = cb_ref[...] + ext_s[0:tm, :] * cw_ref[0:1, :]
    for k in range(1, cw):
        xc = xc + ext_s[k * bsz:k * bsz + tm, :] * cw_ref[k:k + 1, :]
    ext_s[0:tail, :] = ext_s[tm:tm + tail, :]

    neg_lam = -lam_ref[...]
    sp = jnp.maximum(neg_lam, 0.0) + jnp.log1p(jnp.exp(-jnp.abs(neg_lam)))
    rate = sp * (-RG_C * math.log2(math.e))
    xcb = xc.astype(BF16)
    per_block = -(-len(pieces) // nb)
    for n in range(nb):
        for p in range(n * per_block, min((n + 1) * per_block, len(pieces))):
            project_piece(p)
        sl = slice(n * LANES, (n + 1) * LANES)
        g = _dot(xcb[:, sl], wg_ref[n])
        r = _sigmoid(g[:, :LANES] + gab_ref[:, sl])
        ig = _sigmoid(g[:, LANES:] + gxb_ref[:, sl])
        a = jnp.exp2(r * rate[:, sl])
        a_s[:, sl] = a
        u = 1.0 - a * a
        root = jnp.where(u > 0.0, u * lax.rsqrt(u), 0.0)
        b_s[:, sl] = root * (ig * xc[:, sl])

    def step(t, h):
        rows = pl.ds(pl.multiple_of(t * bsz, bsz), bsz)
        h = a_s[rows, :] * h + b_s[rows, :]
        b_s[rows, :] = h
        return h

    carry_s[...] = lax.fori_loop(0, tt, step, carry_s[...], unroll=True)

    z = (jax.nn.gelu(yx_s[slot, :, 0:c].astype(F32)) * b_s[...]).astype(BF16)
    mix = _dot(z, wo_ref[...])
    out_ref[...] = h_ref[...] + _rms(mix, pn_ref[...])


def _a2(h0, pre, w_in, conv_w, conv_b, wg, ga_b, gx_b, lam, w_out, post, *, tt, bsz):
    r, d = h0.shape
    c = w_in.shape[1] // 2
    nb = wg.shape[0]
    cw = conv_w.shape[0]
    tm = tt * bsz
    nblk = r // tm
    row = lambda i: (i, 0)
    fixed2 = lambda i: (0, 0)
    fixed3 = lambda i: (0, 0, 0)
    return pl.pallas_call(
        functools.partial(_a2_body, nb=nb, tt=tt, bsz=bsz, cw=cw),
        grid=(nblk,),
        in_specs=[pl.BlockSpec((tm, d), row),
                  pl.BlockSpec((tm, d), lambda i: (jnp.minimum(i + 1, nblk - 1), 0)),
                  pl.BlockSpec((1, d), fixed2),
                  pl.BlockSpec((d, 2 * c), fixed2, pipeline_mode=pl.Buffered(1)),
                  pl.BlockSpec((cw, c), fixed2), pl.BlockSpec((1, c), fixed2),
                  pl.BlockSpec((nb, LANES, 2 * LANES), fixed3),
                  pl.BlockSpec((1, c), fixed2), pl.BlockSpec((1, c), fixed2), pl.BlockSpec((1, c), fixed2),
                  pl.BlockSpec((c, d), fixed2, pipeline_mode=pl.Buffered(1)), pl.BlockSpec((1, d), fixed2)],
        out_specs=pl.BlockSpec((tm, d), row),
        out_shape=jax.ShapeDtypeStruct((r, d), F32),
        scratch_shapes=[pltpu.VMEM((2, tm, 2 * c), BF16),
                        pltpu.VMEM((tm + (cw - 1) * bsz, c), F32), pltpu.VMEM((tm, c), F32),
                        pltpu.VMEM((tm, c), F32), pltpu.VMEM((bsz, c), F32)],
        compiler_params=_cparams(("arbitrary",)),
        name="a2_rglru",
    )(h0, h0, pre, w_in, conv_w, conv_b, wg, ga_b, gx_b, lam, w_out, post)


def _a3_body(h_ref, pre_ref, wgu_ref, wd_ref, post_ref, out_ref, tr_s, *, npad, bsz, tt, tf):
    i = pl.program_id(0)

    @pl.when(i < npad)
    def _():
        out_ref[...] = jnp.zeros_like(out_ref)

    @pl.when(i >= npad)
    def _():
        h = h_ref[...]
        xn = _rms(h, pre_ref[...]).astype(BF16)
        ff = wd_ref.shape[0]
        acc = None
        for c in range(ff // tf):
            g = _dot(xn, wgu_ref[:, c * tf:(c + 1) * tf])
            u = _dot(xn, wgu_ref[:, ff + c * tf:ff + (c + 1) * tf])
            part = _dot((g * _sigmoid(g) * u).astype(BF16), wd_ref[c * tf:(c + 1) * tf, :])
            acc = part if acc is None else acc + part
        res = h + _rms(acc, post_ref[...])
        for c in range(tr_s.shape[0]):
            tr_s[c] = res[:, c * LANES:(c + 1) * LANES]
        for b in range(bsz):
            for c in range(tr_s.shape[0]):
                out_ref[b, :, c * LANES:(c + 1) * LANES] = tr_s[c, pl.ds(b, tt, stride=bsz), :]


def _a3(h1, pre, w_gu, w_d, post, *, bsz, tt, npad, tf):
    r, d = h1.shape
    tm = tt * bsz
    nblk = r // tm
    fixed = lambda i: (0, 0)
    resident = lambda a: pl.BlockSpec(a.shape, fixed, pipeline_mode=pl.Buffered(1))
    return pl.pallas_call(
        functools.partial(_a3_body, npad=npad, bsz=bsz, tt=tt, tf=tf),
        grid=(npad + nblk,),
        in_specs=[pl.BlockSpec((tm, d), lambda i: (jnp.maximum(i - npad, 0), 0)), pl.BlockSpec((1, d), fixed),
                  resident(w_gu), resident(w_d), pl.BlockSpec((1, d), fixed)],
        out_specs=pl.BlockSpec((bsz, tt, d), lambda i: (0, i, 0)),
        out_shape=jax.ShapeDtypeStruct((bsz, (npad + nblk) * tt, d), F32),
        scratch_shapes=[pltpu.VMEM((d // LANES, tm, LANES), F32)],
        compiler_params=_cparams(("arbitrary",)),
        name="a3_dense_ffn",
    )(h1, pre, w_gu, w_d, post)


def _b1_body(h_ref, kvn_ref, bpre_ref, wkv_ref, latn_ref, wk_ref, wv_ref,
             wdq_ref, qn_ref, wq_ref, tq_ref, tkc_ref, tks_ref, q_ref, k_ref, v_ref, *, nh):
    h = h_ref[...]
    hn = h * lax.rsqrt(jnp.mean(h * h, axis=-1, keepdims=True) + RMS_EPS)
    ukv = (hn * kvn_ref[...]).astype(BF16)
    uq = (hn * bpre_ref[...]).astype(BF16)

    lat = wkv_ref.shape[1] - 2 * HEAD_PAD
    dkv = _dot(ukv, wkv_ref[...])
    ckv = _rms(dkv[:, :lat], latn_ref[...]).astype(BF16)
    kr = dkv[:, lat:lat + HEAD_PAD] * tkc_ref[...] + dkv[:, lat + HEAD_PAD:] * tks_ref[...]
    kn = _dot(ckv, wk_ref[...])
    for hd in range(nh):
        sl = slice(hd * HEAD_PAD, (hd + 1) * HEAD_PAD)
        k_ref[:, sl] = (kn[:, sl] + kr).astype(BF16)
    v_ref[...] = _dot(ckv, wv_ref[...]).astype(BF16)

    cq = _rms(_dot(uq, wdq_ref[...]), qn_ref[...]).astype(BF16)
    qr = _dot(cq, wq_ref[...])
    tq = tq_ref[...]
    for hd in range(nh):
        sl = slice(hd * HEAD_PAD, (hd + 1) * HEAD_PAD)
        q_ref[:, sl] = (qr[:, sl] * tq).astype(BF16)


def _b1(h2, kvn, bpre, wkv, latn, wk, wv, wdq, qn, wq, tq, tkc, tks, *, tm, nh):
    r, d = h2.shape
    nper = tq.shape[0] // tm
    row = lambda i: (i, 0)
    tab = lambda i: (i % nper, 0)
    fixed = lambda i: (0, 0)
    full = lambda a: pl.BlockSpec(a.shape, fixed)
    return pl.pallas_call(
        functools.partial(_b1_body, nh=nh),
        grid=(r // tm,),
        in_specs=[pl.BlockSpec((tm, d), row), full(kvn), full(bpre), full(wkv),
                  full(latn), full(wk), full(wv), full(wdq), full(qn), full(wq),
                  pl.BlockSpec((tm, HEAD_PAD), tab), pl.BlockSpec((tm, HEAD_PAD), tab),
                  pl.BlockSpec((tm, HEAD_PAD), tab)],
        out_specs=[pl.BlockSpec((tm, nh * HEAD_PAD), row), pl.BlockSpec((tm, nh * HEAD_PAD), row),
                   pl.BlockSpec((tm, nh * V_HEAD), row)],
        out_shape=[jax.ShapeDtypeStruct((r, nh * HEAD_PAD), BF16),
                   jax.ShapeDtypeStruct((r, nh * HEAD_PAD), BF16),
                   jax.ShapeDtypeStruct((r, nh * V_HEAD), BF16)],
        compiler_params=_cparams(("arbitrary",)),
        name="b1_qkv_proj",
    )(h2, kvn, bpre, wkv, latn, wk, wv, wdq, qn, wq, tq, tkc, tks)


def _tile2(x):
    return jnp.concatenate([x, x], axis=1)


def _b2_body(q_ref, k_ref, v_ref, o_ref, *scratch, meta0, seq0, nq, qb, kb, nhs):
    pair = 2 * V_HEAD
    t_pad = v_ref.shape[1]
    nsub = qb // kb
    npairs = nhs // 2
    vext_s = scratch[0:npairs]
    m_s = scratch[npairs:npairs + nhs]
    acc_s = scratch[npairs + nhs:npairs + 2 * nhs]
    for pp in range(npairs):
        vext_s[pp][meta0:, 0:pair] = v_ref[0, meta0:, pp * pair:(pp + 1) * pair]
        vext_s[pp][meta0:, pair:] = jnp.ones((t_pad - meta0, pair), BF16)
    o_ref[0, 0:seq0, :] = jnp.zeros((seq0, npairs * pair), o_ref.dtype)
    lane = lax.broadcasted_iota(I32, (qb, pair), 1)
    neg = jnp.finfo(F32).min
    heads = range(nhs)

    def scores(hh, qh, qlo, krows):
        return _dot_t(qh[qlo:], k_ref[0, krows, hh * HEAD_PAD:(hh + 1) * HEAD_PAD])

    def absorb(hh, s, qlo, krows, masked):
        if masked:
            rowi = lax.broadcasted_iota(I32, s.shape, 0)
            coli = lax.broadcasted_iota(I32, s.shape, 1)
            s = jnp.where(coli <= rowi, s, neg)
        m_old = m_s[hh][qlo:, :]
        m_new = jnp.maximum(m_old, jnp.max(s, axis=-1, keepdims=True))
        alpha = jnp.exp2(m_old - m_new)
        p = jnp.exp2(s - jnp.concatenate([m_new] * (s.shape[1] // pair), axis=1))
        acc_s[hh][qlo:, :] = (_tile2(alpha) * acc_s[hh][qlo:, :]
                              + _dot(p.astype(BF16), vext_s[hh // 2][krows, :]))
        m_s[hh][qlo:, :] = m_new

    def qblock(i, _):
        q0 = pl.multiple_of(seq0 + i * qb, ATT_BLK)
        qs = [q_ref[0, pl.ds(q0, qb), hh * HEAD_PAD:(hh + 1) * HEAD_PAD] for hh in heads]
        for hh in heads:
            s = _dot_t(qs[hh], k_ref[0, meta0:seq0, hh * HEAD_PAD:(hh + 1) * HEAD_PAD])
            m = jnp.max(s, axis=-1, keepdims=True)
            p = jnp.exp2(s - m)
            acc_s[hh][...] = _dot(p.astype(BF16), vext_s[hh // 2][meta0:seq0, :])
            m_s[hh][...] = jnp.broadcast_to(m, (qb, pair))

        def kv_step(j, _):
            krows = pl.ds(pl.multiple_of(seq0 + j * kb, ATT_BLK), kb)
            for hh in heads:
                absorb(hh, scores(hh, qs[hh], 0, krows), 0, krows, False)
            return 0

        lax.fori_loop(0, i * nsub, kv_step, 0)
        for jj in range(nsub):
            krows = pl.ds(pl.multiple_of(q0 + jj * kb, ATT_BLK), kb)
            for hh in heads:
                absorb(hh, scores(hh, qs[hh], jj * kb, krows), jj * kb, krows, True)
        for pp in range(npairs):
            res = []
            for hh in (2 * pp, 2 * pp + 1):
                acc = acc_s[hh][...]
                res.append(acc[:, :pair] / acc[:, pair:])
            o_ref[0, pl.ds(q0, qb), pp * pair:(pp + 1) * pair] = jnp.where(
                lane < V_HEAD, res[0], res[1]).astype(o_ref.dtype)
        return 0

    lax.fori_loop(0, nq, qblock, 0)


def _b2(q, k, v, *, nh, meta0, seq0, seq, qb, nhs):
    bsz, t_pad, _ = q.shape
    qk = pl.BlockSpec((1, t_pad, nhs * HEAD_PAD), lambda b, p: (b, 0, p))
    vo = pl.BlockSpec((1, t_pad, nhs * V_HEAD), lambda b, p: (b, 0, p))
    return pl.pallas_call(
        functools.partial(_b2_body, meta0=meta0, seq0=seq0, nq=seq // qb, qb=qb, kb=min(ATT_KBLK, qb), nhs=nhs),
        grid=(bsz, nh // nhs),
        in_specs=[qk, qk, vo],
        out_specs=vo,
        out_shape=jax.ShapeDtypeStruct(v.shape, BF16),
        scratch_shapes=([pltpu.VMEM((t_pad, 4 * V_HEAD), BF16)] * (nhs // 2)
                        + [pltpu.VMEM((qb, 2 * V_HEAD), F32)] * nhs
                        + [pltpu.VMEM((qb, 4 * V_HEAD), F32)] * nhs),
        compiler_params=_cparams(("arbitrary", "arbitrary")),
        name="b2_attention",
    )(q, k, v)


B3_TILES = 4


def _b3_body(*refs, tm):
    o_refs = refs[0:2 * B3_TILES:2]
    h_refs = refs[1:2 * B3_TILES:2]
    wo_ref, post_ref, mpre_ref, rw_ref, rb_ref, h3_ref, xn_ref, route_ref, cnt_ref, cnt_s = refs[2 * B3_TILES:]
    tiles = range(B3_TILES)

    @pl.when(pl.program_id(0) == 0)
    def _():
        cnt_s[...] = jnp.zeros_like(cnt_s)

    mix = [_dot(o_refs[t][0], wo_ref[...]) for t in tiles]
    h3 = [h_refs[t][0] + _rms(mix[t], post_ref[...]) for t in tiles]
    xn = [_rms(h3[t], mpre_ref[...]) for t in tiles]
    logits = [_dot(xn[t].astype(BF16), rw_ref[...]) + rb_ref[...] for t in tiles]
    lane = lax.broadcasted_iota(I32, logits[0].shape, 1).astype(F32)
    big = float(LANES)
    count = cnt_s[...]
    for t in tiles:
        rows = slice(t * tm, (t + 1) * tm)
        h3_ref[rows, :] = h3[t]
        xn_ref[rows, :] = xn[t]
        m1 = jnp.max(logits[t], axis=-1, keepdims=True)
        i1 = jnp.min(jnp.where(logits[t] == m1, lane, big), axis=-1, keepdims=True)
        rest = jnp.where(lane == i1, -jnp.inf, logits[t])
        m2 = jnp.max(rest, axis=-1, keepdims=True)
        i2 = jnp.min(jnp.where(rest == m2, lane, big), axis=-1, keepdims=True)
        e2 = jnp.exp(m2 - m1)
        g1 = 1.0 / (1.0 + e2)
        g2 = e2 / (1.0 + e2)
        route_ref[rows, :] = jnp.where(lane == 0.0, g1, jnp.where(lane == 1.0, g2,
                                       jnp.where(lane == 2.0, i1, jnp.where(lane == 3.0, i2, 0.0))))
        onehot = jnp.where((lane == i1) | (lane == i2), 1.0, 0.0)
        count = count + jnp.sum(onehot, axis=0, keepdims=True)
    cnt_s[...] = count
    cnt_ref[...] = jnp.broadcast_to(count, cnt_ref.shape)


def _b3(o, h2, w_o, post, mpre, rw, rb, *, tm, blk0, nper):
    bsz, _, d = h2.shape
    nblk = bsz * nper
    assert nblk % B3_TILES == 0
    src = lambda t: (lambda i: ((B3_TILES * i + t) // nper, blk0 + (B3_TILES * i + t) % nper, 0))
    dst = lambda i: (i, 0)
    fixed = lambda i: (0, 0)
    rows = nblk * tm
    step = B3_TILES * tm
    tile_specs = []
    for t in range(B3_TILES):
        tile_specs += [pl.BlockSpec((1, tm, o.shape[2]), src(t)), pl.BlockSpec((1, tm, d), src(t))]
    return pl.pallas_call(
        functools.partial(_b3_body, tm=tm),
        grid=(nblk // B3_TILES,),
        in_specs=tile_specs + [pl.BlockSpec(w_o.shape, fixed), pl.BlockSpec((1, d), fixed),
                               pl.BlockSpec((1, d), fixed), pl.BlockSpec(rw.shape, fixed),
                               pl.BlockSpec((1, LANES), fixed)],
        out_specs=[pl.BlockSpec((step, d), dst), pl.BlockSpec((step, d), dst), pl.BlockSpec((step, LANES), dst),
                   pl.BlockSpec((8, LANES), fixed)],
        out_shape=[jax.ShapeDtypeStruct((rows, d), F32), jax.ShapeDtypeStruct((rows, d), F32),
                   jax.ShapeDtypeStruct((rows, LANES), F32), jax.ShapeDtypeStruct((8, LANES), F32)],
        scratch_shapes=[pltpu.VMEM((1, LANES), F32)],
        compiler_params=_cparams(("arbitrary",)),
        name="b3_oproj_router",
    )(*([o, h2] * B3_TILES), w_o, post, mpre, rw, rb)


B4_TILES = 4


def _b4_body(route_ref, cnt_ref, meta_ref, run_s, off_s, *, tile, tm):
    lane = lax.broadcasted_iota(I32, (tm, LANES), 1).astype(F32)

    @pl.when(pl.program_id(0) == 0)
    def _():
        total = cnt_ref[0:1, :]
        ntiles = jnp.floor((total + (tile - 1)) * (1.0 / tile))
        rr = lax.broadcasted_iota(I32, (LANES, LANES), 0)
        cc = lax.broadcasted_iota(I32, (LANES, LANES), 1)
        upper = jnp.where(rr < cc, 1.0, 0.0).astype(BF16)
        before = _dot(jnp.broadcast_to(ntiles, (8, LANES)).astype(BF16), upper)
        off_s[...] = before[0:1, :] * float(tile)
        run_s[...] = jnp.zeros_like(run_s)

    rr = lax.broadcasted_iota(I32, (tm, tm), 0)
    cc = lax.broadcasted_iota(I32, (tm, tm), 1)
    lower = jnp.where(cc < rr, 1.0, 0.0).astype(BF16)
    base = run_s[...] + off_s[...]
    for t in range(B4_TILES):
        route = route_ref[t * tm:(t + 1) * tm, :]
        i1 = jnp.sum(jnp.where(lane == 2.0, route, 0.0), axis=-1, keepdims=True)
        i2 = jnp.sum(jnp.where(lane == 3.0, route, 0.0), axis=-1, keepdims=True)
        oh1 = lane == i1
        oh2 = lane == i2
        onehot = jnp.where(oh1 | oh2, 1.0, 0.0)
        pos = _dot(lower, onehot.astype(BF16)) + base
        p1 = jnp.sum(jnp.where(oh1, pos, 0.0), axis=-1, keepdims=True)
        p2 = jnp.sum(jnp.where(oh2, pos, 0.0), axis=-1, keepdims=True)
        packed = jnp.where(lane == 0.0, p1, jnp.where(lane == 1.0, p2, 0.0))
        meta_ref[t] = jnp.transpose(packed)[0:8, :].astype(I32)
        base = base + jnp.sum(onehot, axis=0, keepdims=True)
    run_s[...] = base - off_s[...]


def _b4(route, cnt, *, tm, tile):
    nblk = route.shape[0] // tm
    assert nblk % B4_TILES == 0
    return pl.pallas_call(
        functools.partial(_b4_body, tile=tile, tm=tm),
        grid=(nblk // B4_TILES,),
        in_specs=[pl.BlockSpec((B4_TILES * tm, LANES), lambda i: (i, 0)),
                  pl.BlockSpec((8, LANES), lambda i: (0, 0))],
        out_specs=pl.BlockSpec((B4_TILES, 8, tm), lambda i: (i, 0, 0)),
        out_shape=jax.ShapeDtypeStruct((nblk, 8, tm), I32),
        scratch_shapes=[pltpu.VMEM((1, LANES), F32), pltpu.VMEM((1, LANES), F32)],
        compiler_params=_cparams(("arbitrary",)),
        name="b4_route_rank",
    )(route, cnt)


def _row(ref, pos):
    return ref.at[pos >> 3, pl.ds(pos & (SUBLANES - 1), 1)]


DISPATCH_BUFS = 3


def _b5_body(meta_ref, xn_ref, xg_in_ref, xg_ref, buf_s, in_sem, out_sem, *, tm):
    del xg_in_ref
    i = pl.program_id(0)
    n = pl.num_programs(0)
    groups = tm // SUBLANES
    slot = i % DISPATCH_BUFS
    nxt = (i + 1) % DISPATCH_BUFS

    def fetch(t, s):
        return pltpu.make_async_copy(xn_ref.at[pl.ds(t * groups, groups)], buf_s.at[s], in_sem.at[s])

    def drain(s):
        for k in range(TOP_K):
            pltpu.make_async_copy(buf_s.at[s], xg_ref.at[pl.ds(0, groups)], out_sem.at[s]).wait()

    @pl.when(i == 0)
    def _():
        fetch(0, 0).start()

    @pl.when(i >= DISPATCH_BUFS - 1)
    def _():
        drain(nxt)

    @pl.when(i + 1 < n)
    def _():
        fetch(i + 1, nxt).start()

    fetch(i, slot).wait()

    def issue(g, _):
        for u in range(SUBLANES):
            for k in range(TOP_K):
                pos = meta_ref[k * tm + g * SUBLANES + u]
                pltpu.make_async_copy(buf_s.at[slot, g, pl.ds(u, 1)], _row(xg_ref, pos), out_sem.at[slot]).start()
        return 0

    lax.fori_loop(0, groups, issue, 0)

    @pl.when(i == n - 1)
    def _():
        for back in range(DISPATCH_BUFS - 1):
            @pl.when(i >= back)
            def _():
                drain((i - back) % DISPATCH_BUFS)


def _b5(meta, xn, xg0, *, tm):
    nblk = meta.shape[0] // (TOP_K * tm)
    return pl.pallas_call(
        functools.partial(_b5_body, tm=tm),
        grid=(nblk,),
        in_specs=[pl.BlockSpec((TOP_K * tm,), lambda i: (i,), memory_space=pltpu.SMEM),
                  pl.BlockSpec(memory_space=pl.ANY),
                  pl.BlockSpec(memory_space=pl.ANY)],
        out_specs=pl.BlockSpec(memory_space=pl.ANY),
        scratch_shapes=[pltpu.VMEM((DISPATCH_BUFS, tm // SUBLANES) + xn.shape[1:], xn.dtype),
                        pltpu.SemaphoreType.DMA((DISPATCH_BUFS,)),
                        pltpu.SemaphoreType.DMA((DISPATCH_BUFS,))],
        out_shape=jax.ShapeDtypeStruct(xg0.shape, xg0.dtype),
        input_output_aliases={2: 0},
        compiler_params=_cparams(("arbitrary",)),
        name="b5_dispatch",
    )(meta, xn, xg0)


def _b6_body(te_ref, tv_ref, x_ref, wg_ref, wu_ref, wd_ref, y_ref, xb_s, acc_s):
    j = pl.program_id(0)
    f = pl.program_id(1)

    @pl.when(f == 0)
    def _():
        xb_s[...] = x_ref[...].astype(BF16)
        acc_s[...] = jnp.zeros_like(acc_s)

    @pl.when(tv_ref[j] > 0)
    def _():
        xb = xb_s[...]
        g = _dot(xb, wg_ref[0])
        u = _dot(xb, wu_ref[0])
        acc_s[...] += _dot((g * _sigmoid(g) * u).astype(BF16), wd_ref[0])

    @pl.when(f == pl.num_programs(1) - 1)
    def _():
        y_ref[...] = acc_s[...]


def _b6(te, tv, xg, w_gu, w_d, *, tm, tf):
    rg, d = xg.shape
    eff = w_d.shape[1]
    nf = eff // tf
    nt = rg // tm
    return pl.pallas_call(
        _b6_body,
        grid_spec=pltpu.PrefetchScalarGridSpec(
            num_scalar_prefetch=2,
            grid=(nt, nf),
            in_specs=[pl.BlockSpec((tm, d), lambda j, f, te, tv: (j, 0)),
                      pl.BlockSpec((1, d, tf), lambda j, f, te, tv: (te[j], 0, f)),
                      pl.BlockSpec((1, d, tf), lambda j, f, te, tv: (te[j], 0, nf + f)),
                      pl.BlockSpec((1, tf, d), lambda j, f, te, tv: (te[j], f, 0))],
            out_specs=pl.BlockSpec((tm, d), lambda j, f, te, tv: (j, 0)),
            scratch_shapes=[pltpu.VMEM((tm, d), BF16), pltpu.VMEM((tm, d), F32)]),
        out_shape=jax.ShapeDtypeStruct((rg, d), F32),
        compiler_params=_cparams(("arbitrary", "arbitrary")),
        name="b6_moe_ffn",
    )(te, tv, xg, w_gu, w_gu, w_d)


def _b7_body(meta_ref, meta_next_ref, route_ref, h_ref, post_ref, yg_ref, out_ref, buf_s, sem, *, tm):
    i = pl.program_id(0)
    slot = i % 2

    def gather_tile(mref, s):
        def issue(g, _):
            for u in range(SUBLANES):
                for k in range(TOP_K):
                    pos = mref[k * tm + g * SUBLANES + u]
                    pltpu.make_async_copy(_row(yg_ref, pos), buf_s.at[s, k, g, pl.ds(u, 1)], sem.at[s]).start()
            return 0

        lax.fori_loop(0, tm // SUBLANES, issue, 0)

    @pl.when(i == 0)
    def _():
        gather_tile(meta_ref, 0)

    @pl.when(i + 1 < pl.num_programs(0))
    def _():
        gather_tile(meta_next_ref, 1 - slot)

    for k in range(TOP_K):
        pltpu.make_async_copy(yg_ref.at[pl.ds(0, tm // SUBLANES)], buf_s.at[slot, k], sem.at[slot]).wait()
    route = route_ref[...]
    d = out_ref.shape[1]
    y = route[:, 0:1] * buf_s[slot, 0].reshape(tm, d) + route[:, 1:2] * buf_s[slot, 1].reshape(tm, d)
    out_ref[...] = h_ref[...] + _rms(y, post_ref[...])


def _b7(meta, route, h3, post, yg, *, tm):
    nblk = meta.shape[0] // (TOP_K * tm)
    d = h3.shape[1]
    row = lambda i: (i, 0)
    return pl.pallas_call(
        functools.partial(_b7_body, tm=tm),
        grid=(nblk,),
        in_specs=[pl.BlockSpec((TOP_K * tm,), lambda i: (i,), memory_space=pltpu.SMEM),
                  pl.BlockSpec((TOP_K * tm,), lambda i: (jnp.minimum(i + 1, nblk - 1),),
                               memory_space=pltpu.SMEM),
                  pl.BlockSpec((tm, LANES), row), pl.BlockSpec((tm, d), row),
                  pl.BlockSpec((1, d), lambda i: (0, 0)),
                  pl.BlockSpec(memory_space=pl.ANY)],
        out_specs=pl.BlockSpec((tm, d), row),
        scratch_shapes=[pltpu.VMEM((2, TOP_K, tm // SUBLANES, SUBLANES, d), F32),
                        pltpu.SemaphoreType.DMA((2,))],
        out_shape=jax.ShapeDtypeStruct((nblk * tm, d), F32),
        compiler_params=_cparams(("arbitrary",)),
        name="b7_combine",
    )(meta, meta, route, h3, post, yg)


def _pick(n, target, mult):
    best = mult
    for cand in range(mult, min(n, target) + 1, mult):
        if n % cand == 0:
            best = cand
    return best


def kernel(x, meta_tokens, a_pre_norm, a_w_in, a_conv_w, a_conv_b, a_gate_a_w, a_gate_a_b, a_gate_x_w, a_gate_x_b, a_lambda, a_w_out, a_post_norm, kv_norm, kv_w_dkv, kv_latent_norm, kv_w_ukv, b_pre_norm, b_w_dq, b_q_norm, b_w_uq, b_w_o, b_post_norm, f_pre_norm, f_w_gate_up, f_w_down, f_post_norm, m_pre_norm, m_router_w, m_router_b, m_w_gate_up, m_w_down, m_post_norm):
    bsz, seq, d = x.shape
    n_meta = meta_tokens.shape[0]
    assert a_pre_norm.shape[0] == 1 and b_pre_norm.shape[0] == 1, "one RG-LRU layer + one MLA layer"
    t_all = n_meta + seq
    assert t_all % TT == 0 and seq % ATT_BLK == 0 and bsz % 8 == 0 and n_meta % 16 == 0
    rows = t_all * bsz
    row = lambda v: v.reshape(1, -1).astype(F32)
    bf = lambda w: w.astype(BF16)

    h0 = jnp.concatenate([jnp.broadcast_to(meta_tokens.astype(F32)[:, None, :], (n_meta, bsz, d)),
                          jnp.transpose(x, (1, 0, 2))], axis=0).reshape(rows, d)

    wg = bf(jnp.concatenate([a_gate_a_w[0], a_gate_x_w[0]], axis=-1))
    h1 = _a2(h0, row(a_pre_norm[0]), bf(a_w_in[0]), a_conv_w[0].astype(F32), row(a_conv_b[0]), wg,
             row(a_gate_a_b[0]), row(a_gate_x_b[0]), row(a_lambda[0]), bf(a_w_out[0]), row(a_post_norm[0]),
             tt=TT_SMALL, bsz=bsz)
    d_ff = f_w_down.shape[1]
    h2 = _a3(h1, row(f_pre_norm[0]), bf(f_w_gate_up[0]), bf(f_w_down[0]), row(f_post_norm[0]),
             bsz=bsz, tt=TT, npad=PAD_T // TT, tf=_pick(d_ff, 1024, LANES))
    t_pad = PAD_T + t_all
    seq0 = PAD_T + n_meta
    assert seq0 % ATT_BLK == 0 and t_pad % (TT_SMALL * bsz) == 0

    kv_lora = kv_latent_norm.shape[0]
    nh = b_w_uq.shape[2] // (QK_NOPE + QK_ROPE)
    hr = QK_ROPE // 2
    scale = (QK_NOPE + QK_ROPE) ** -0.5 * math.log2(math.e)
    inv = ROPE_THETA ** (-jnp.arange(0, QK_ROPE, 2, dtype=F32) / QK_ROPE)
    pos = jnp.maximum(jnp.arange(t_pad, dtype=F32) - PAD_T, 0.0)
    ang = pos[:, None] * inv[None, :]
    cos, sin = jnp.cos(ang), jnp.sin(ang)
    one = jnp.ones((t_pad, QK_NOPE), F32)
    zero = jnp.zeros((t_pad, QK_NOPE), F32)
    tq = scale * jnp.concatenate([one, cos, cos, -sin, sin], axis=1)
    tkc = jnp.concatenate([zero, cos, cos, cos, cos], axis=1)
    tks = jnp.concatenate([zero, -sin, sin, -sin, sin], axis=1)

    w_c = kv_w_dkv[:, :kv_lora]
    k1 = kv_w_dkv[:, kv_lora:kv_lora + hr]
    k2 = kv_w_dkv[:, kv_lora + hr:]
    zk = jnp.zeros((d, QK_NOPE), F32)
    wka = jnp.concatenate([zk, k1, k2, k1, k2], axis=1)
    wkb = jnp.concatenate([zk, k2, k1, k2, k1], axis=1)
    ukv = kv_w_ukv.reshape(kv_lora, nh, QK_NOPE + V_HEAD)
    wk = jnp.concatenate([ukv[:, :, :QK_NOPE], jnp.zeros((kv_lora, nh, HEAD_PAD - QK_NOPE), F32)],
                         axis=2).reshape(kv_lora, nh * HEAD_PAD)
    wv = ukv[:, :, QK_NOPE:].reshape(kv_lora, nh * V_HEAD)
    uq = b_w_uq[0].reshape(-1, nh, QK_NOPE + QK_ROPE)
    qn_, q1, q2 = uq[:, :, :QK_NOPE], uq[:, :, QK_NOPE:QK_NOPE + hr], uq[:, :, QK_NOPE + hr:]
    wq = jnp.concatenate([qn_, q1, q2, q2, q1], axis=2).reshape(-1, nh * HEAD_PAD)
    wkv = bf(jnp.concatenate([w_c, wka, wkb], axis=1))
    q, k, v = _b1(h2.reshape(bsz * t_pad, d), row(kv_norm), row(b_pre_norm[0]), wkv,
                  row(kv_latent_norm), bf(wk), bf(wv), bf(b_w_dq[0]), row(b_q_norm[0]), bf(wq), tq, tkc, tks,
                  tm=TT_SMALL * bsz, nh=nh)

    o = _b2(q.reshape(bsz, t_pad, nh * HEAD_PAD), k.reshape(bsz, t_pad, nh * HEAD_PAD),
            v.reshape(bsz, t_pad, nh * V_HEAD), nh=nh, meta0=PAD_T, seq0=seq0, seq=seq,
            qb=_pick(seq, ATT_QBLK, ATT_BLK), nhs=ATT_HEADS)

    rows_seq = seq * bsz
    tm1 = ATT_BLK
    assert rows_seq % MOE_TILE == 0
    ne = m_router_w.shape[2]
    rw = bf(jnp.concatenate([m_router_w[0], jnp.zeros((d, LANES - ne), F32)], axis=1))
    rb = jnp.concatenate([m_router_b[0].astype(F32), jnp.full((LANES - ne,), -1e30, F32)]).reshape(1, LANES)
    h3, xn, route, cnt = _b3(o, h2, bf(b_w_o[0]), row(b_post_norm[0]), row(m_pre_norm[0]), rw, rb,
                             tm=tm1, blk0=seq0 // tm1, nper=seq // tm1)

    tmg = MOE_TILE
    meta = _b4(route, cnt, tm=tm1, tile=tmg)[:, :TOP_K, :].reshape(-1)
    counts = cnt[0, :ne].astype(I32)
    padded = ((counts + tmg - 1) // tmg) * tmg
    ends = jnp.cumsum(padded)
    nt = (TOP_K * rows_seq) // tmg + ne
    tile_start = jnp.arange(nt, dtype=I32) * tmg
    tv = (tile_start < ends[-1]).astype(I32)
    last = jnp.maximum(ends[-1] // tmg - 1, 0)
    te_raw = jnp.sum((tile_start[:, None] >= ends[None, :]).astype(I32), axis=1)
    te = jnp.minimum(jnp.where(tv > 0, te_raw, te_raw[last]), ne - 1).astype(I32)

    grp = lambda a: a.reshape(a.shape[0] // SUBLANES, SUBLANES, d)
    xg = _b5(meta, grp(xn), jnp.zeros((nt * tmg // SUBLANES, SUBLANES, d), F32), tm=tm1)
    e_ff = m_w_down.shape[2]
    yg = _b6(te, tv, xg.reshape(nt * tmg, d), bf(m_w_gate_up[0]), bf(m_w_down[0]), tm=tmg,
             tf=_pick(e_ff, MOE_FF_CHUNK, LANES))
    h4 = _b7(meta, route, h3, row(m_post_norm[0]), grp(yg), tm=tm1)
    return h4.reshape(bsz, seq, d)
```

```python
import functools
import math

import jax
import jax.numpy as jnp
from jax import lax
from jax.experimental import pallas as pl
from jax.experimental.pallas import tpu as pltpu

F32 = jnp.float32
BF16 = jnp.bfloat16
I32 = jnp.int32

RMS_EPS = 1e-6
RG_C = 8.0
ROPE_THETA = 10000.0
QK_NOPE = 64
QK_ROPE = 32
V_HEAD = 64
HEAD_PAD = 128
TOP_K = 2
LANES = 128
SUBLANES = 8

TT = 48
TT_SMALL = 24
PAD_T = 240
ATT_BLK = 256
ATT_KBLK = 256
ATT_QBLK = 2048
ATT_HEADS = 4
MOE_TILE = 512
MOE_FF_CHUNK = 1792
VMEM_LIMIT = 56 * 1024 * 1024


def _cparams(sem):
    return pltpu.CompilerParams(dimension_semantics=sem, vmem_limit_bytes=VMEM_LIMIT)


def _rms(x, g):
    ms = jnp.mean(x * x, axis=-1, keepdims=True)
    return x * lax.rsqrt(ms + RMS_EPS) * g


def _sigmoid(x):
    return 0.5 * jnp.tanh(0.5 * x) + 0.5


def _dot(a, b):
    return jnp.dot(a, b, preferred_element_type=F32)


def _dot_t(a, b):
    return lax.dot_general(a, b, (((1,), (1,)), ((), ())), preferred_element_type=F32)


def _a2_body(h_ref, hnext_ref, pre_ref, win_ref, cw_ref, cb_ref, wg_ref, gab_ref, gxb_ref, lam_ref,
             wo_ref, pn_ref, out_ref, yx_s, ext_s, a_s, b_s, carry_s, *, nb, tt, bsz, cw):
    tm = tt * bsz
    tail = (cw - 1) * bsz
    c = ext_s.shape[1]
    i = pl.program_id(0)
    slot = i % 2

    def project(href, s):
        xn = _rms(href[...], pre_ref[...]).astype(BF16)
        yx_s[s, :, 0:c] = _dot(xn, win_ref[:, 0:c]).astype(BF16)
        yx_s[s, :, c:] = _dot(xn, win_ref[:, c:]).astype(BF16)

    @pl.when(i == 0)
    def _():
        ext_s[0:tail, :] = jnp.zeros((tail, c), F32)
        carry_s[...] = jnp.zeros_like(carry_s)
        project(h_ref, 0)

    ext_s[tail:tail + tm, :] = yx_s[slot, :, c:].astype(F32)

    xn_next = _rms(hnext_ref[...], pre_ref[...]).astype(BF16)
    pw = 2 * LANES
    pieces = [(s0, min(s0 + pw, 2 * c)) for s0 in range(0, 2 * c, pw)]

    def project_piece(p):
        lo, hi = pieces[p]
        yx_s[1 - slot, :, lo:hi] = _dot(xn_next, win_ref[:, lo:hi]).astype(BF16)

    xc = cb_ref[...] + ext_s[0:tm, :] * cw_ref[0:1, :]
    for k in range(1, cw):
        xc = xc + ext_s[k * bsz:k * bsz + tm, :] * cw_ref[k:k + 1, :]
    ext_s[0:tail, :] = ext_s[tm:tm + tail, :]

    neg_lam = -lam_ref[...]
    sp = jnp.maximum(neg_lam, 0.0) + jnp.log1p(jnp.exp(-jnp.abs(neg_lam)))
    rate = sp * (-RG_C * math.log2(math.e))
    xcb = xc.astype(BF16)
    per_block = -(-len(pieces) // nb)
    for n in range(nb):
        for p in range(n * per_block, min((n + 1) * per_block, len(pieces))):
            project_piece(p)
        sl = slice(n * LANES, (n + 1) * LANES)
        g = _dot(xcb[:, sl], wg_ref[n])
        r = _sigmoid(g[:, :LANES] + gab_ref[:, sl])
        ig = _sigmoid(g[:, LANES:] + gxb_ref[:, sl])
        a = jnp.exp2(r * rate[:, sl])
        a_s[:, sl] = a
        u = 1.0 - a * a
        root = jnp.where(u > 0.0, u * lax.rsqrt(u), 0.0)
        b_s[:, sl] = root * (ig * xc[:, sl])

    def step(t, h):
        rows = pl.ds(pl.multiple_of(t * bsz, bsz), bsz)
        h = a_s[rows, :] * h + b_s[rows, :]
        b_s[rows, :] = h
        return h

    carry_s[...] = lax.fori_loop(0, tt, step, carry_s[...], unroll=True)

    z = (jax.nn.gelu(yx_s[slot, :, 0:c].astype(F32)) * b_s[...]).astype(BF16)
    mix = _dot(z, wo_ref[...])
    out_ref[...] = h_ref[...] + _rms(mix, pn_ref[...])


def _a2(h0, pre, w_in, conv_w, conv_b, wg, ga_b, gx_b, lam, w_out, post, *, tt, bsz):
    r, d = h0.shape
    c = w_in.shape[1] // 2
    nb = wg.shape[0]
    cw = conv_w.shape[0]
    tm = tt * bsz
    nblk = r // tm
    row = lambda i: (i, 0)
    fixed2 = lambda i: (0, 0)
    fixed3 = lambda i: (0, 0, 0)
    return pl.pallas_call(
        functools.partial(_a2_body, nb=nb, tt=tt, bsz=bsz, cw=cw),
        grid=(nblk,),
        in_specs=[pl.BlockSpec((tm, d), row),
                  pl.BlockSpec((tm, d), lambda i: (jnp.minimum(i + 1, nblk - 1), 0)),
                  pl.BlockSpec((1, d), fixed2),
                  pl.BlockSpec((d, 2 * c), fixed2, pipeline_mode=pl.Buffered(1)),
                  pl.BlockSpec((cw, c), fixed2), pl.BlockSpec((1, c), fixed2),
                  pl.BlockSpec((nb, LANES, 2 * LANES), fixed3),
                  pl.BlockSpec((1, c), fixed2), pl.BlockSpec((1, c), fixed2), pl.BlockSpec((1, c), fixed2),
                  pl.BlockSpec((c, d), fixed2, pipeline_mode=pl.Buffered(1)), pl.BlockSpec((1, d), fixed2)],
        out_specs=pl.BlockSpec((tm, d), row),
        out_shape=jax.ShapeDtypeStruct((r, d), F32),
        scratch_shapes=[pltpu.VMEM((2, tm, 2 * c), BF16),
                        pltpu.VMEM((tm + (cw - 1) * bsz, c), F32), pltpu.VMEM((tm, c), F32),
                        pltpu.VMEM((tm, c), F32), pltpu.VMEM((bsz, c), F32)],
        compiler_params=_cparams(("arbitrary",)),
        name="a2_rglru",
    )(h0, h0, pre, w_in, conv_w, conv_b, wg, ga_b, gx_b, lam, w_out, post)


A3_HALVES = 2


def _a3_body(h_ref, pre_ref, wgu_ref, wd_ref, post_ref, out_ref, tr_s, *, npad, bsz, tt, tf):
    i = pl.program_id(0)

    @pl.when(i < npad)
    def _():
        out_ref[...] = jnp.zeros_like(out_ref)

    @pl.when(i >= npad)
    def _():
        halves = range(A3_HALVES)
        th = tt // A3_HALVES
        rows = [slice(s * th * bsz, (s + 1) * th * bsz) for s in halves]
        h = [h_ref[rows[s], :] for s in halves]
        xn = [_rms(h[s], pre_ref[...]).astype(BF16) for s in halves]
        ff = wd_ref.shape[0]
        acc = [None for _ in halves]
        for c in range(ff // tf):
            g = [_dot(xn[s], wgu_ref[:, c * tf:(c + 1) * tf]) for s in halves]
            u = [_dot(xn[s], wgu_ref[:, ff + c * tf:ff + (c + 1) * tf]) for s in halves]
            for s in halves:
                part = _dot((g[s] * _sigmoid(g[s]) * u[s]).astype(BF16), wd_ref[c * tf:(c + 1) * tf, :])
                acc[s] = part if acc[s] is None else acc[s] + part
        for s in halves:
            res = h[s] + _rms(acc[s], post_ref[...])
            for c in range(tr_s.shape[0]):
                tr_s[c, rows[s], :] = res[:, c * LANES:(c + 1) * LANES]
            for b in range(bsz):
                for c in range(tr_s.shape[0]):
                    out_ref[b, s * th:(s + 1) * th, c * LANES:(c + 1) * LANES] = (
                        tr_s[c, pl.ds(s * th * bsz + b, th, stride=bsz), :])


def _a3(h1, pre, w_gu, w_d, post, *, bsz, tt, npad, tf):
    r, d = h1.shape
    tm = tt * bsz
    nblk = r // tm
    fixed = lambda i: (0, 0)
    resident = lambda a: pl.BlockSpec(a.shape, fixed, pipeline_mode=pl.Buffered(1))
    return pl.pallas_call(
        functools.partial(_a3_body, npad=npad, bsz=bsz, tt=tt, tf=tf),
        grid=(npad + nblk,),
        in_specs=[pl.BlockSpec((tm, d), lambda i: (jnp.maximum(i - npad, 0), 0)), pl.BlockSpec((1, d), fixed),
                  resident(w_gu), resident(w_d), pl.BlockSpec((1, d), fixed)],
        out_specs=pl.BlockSpec((bsz, tt, d), lambda i: (0, i, 0)),
        out_shape=jax.ShapeDtypeStruct((bsz, (npad + nblk) * tt, d), F32),
        scratch_shapes=[pltpu.VMEM((d // LANES, tm, LANES), F32)],
        compiler_params=_cparams(("arbitrary",)),
        name="a3_dense_ffn",
    )(h1, pre, w_gu, w_d, post)


def _b1_body(h_ref, kvn_ref, bpre_ref, wkv_ref, latn_ref, wk_ref, wv_ref,
             wdq_ref, qn_ref, wq_ref, tq_ref, tkc_ref, tks_ref, q_ref, k_ref, v_ref, *, nh):
    h = h_ref[...]
    hn = h * lax.rsqrt(jnp.mean(h * h, axis=-1, keepdims=True) + RMS_EPS)
    ukv = (hn * kvn_ref[...]).astype(BF16)
    uq = (hn * bpre_ref[...]).astype(BF16)

    lat = wkv_ref.shape[1] - 2 * HEAD_PAD
    dkv = _dot(ukv, wkv_ref[...])
    ckv = _rms(dkv[:, :lat], latn_ref[...]).astype(BF16)
    kr = dkv[:, lat:lat + HEAD_PAD] * tkc_ref[...] + dkv[:, lat + HEAD_PAD:] * tks_ref[...]
    kn = _dot(ckv, wk_ref[...])
    for hd in range(nh):
        sl = slice(hd * HEAD_PAD, (hd + 1) * HEAD_PAD)
        k_ref[:, sl] = (kn[:, sl] + kr).astype(BF16)
    v_ref[...] = _dot(ckv, wv_ref[...]).astype(BF16)

    cq = _rms(_dot(uq, wdq_ref[...]), qn_ref[...]).astype(BF16)
    qr = _dot(cq, wq_ref[...])
    tq = tq_ref[...]
    for hd in range(nh):
        sl = slice(hd * HEAD_PAD, (hd + 1) * HEAD_PAD)
        q_ref[:, sl] = (qr[:, sl] * tq).astype(BF16)


def _b1(h2, kvn, bpre, wkv, latn, wk, wv, wdq, qn, wq, tq, tkc, tks, *, tm, nh):
    r, d = h2.shape
    nper = tq.shape[0] // tm
    row = lambda i: (i, 0)
    tab = lambda i: (i % nper, 0)
    fixed = lambda i: (0, 0)
    full = lambda a: pl.BlockSpec(a.shape, fixed)
    return pl.pallas_call(
        functools.partial(_b1_body, nh=nh),
        grid=(r // tm,),
        in_specs=[pl.BlockSpec((tm, d), row), full(kvn), full(bpre), full(wkv),
                  full(latn), full(wk), full(wv), full(wdq), full(qn), full(wq),
                  pl.BlockSpec((tm, HEAD_PAD), tab), pl.BlockSpec((tm, HEAD_PAD), tab),
                  pl.BlockSpec((tm, HEAD_PAD), tab)],
        out_specs=[pl.BlockSpec((tm, nh * HEAD_PAD), row), pl.BlockSpec((tm, nh * HEAD_PAD), row),
                   pl.BlockSpec((tm, nh * V_HEAD), row)],
        out_shape=[jax.ShapeDtypeStruct((r, nh * HEAD_PAD), BF16),
                   jax.ShapeDtypeStruct((r, nh * HEAD_PAD), BF16),
                   jax.ShapeDtypeStruct((r, nh * V_HEAD), BF16)],
        compiler_params=_cparams(("arbitrary",)),
        name="b1_qkv_proj",
    )(h2, kvn, bpre, wkv, latn, wk, wv, wdq, qn, wq, tq, tkc, tks)


def _tile2(x):
    return jnp.concatenate([x, x], axis=1)


def _b2_body(q_ref, k_ref, v_ref, o_ref, *scratch, meta0, seq0, nq, qb, kb, nhs):
    pair = 2 * V_HEAD
    t_pad = v_ref.shape[1]
    nsub = qb // kb
    npairs = nhs // 2
    vext_s = scratch[0:npairs]
    m_s = scratch[npairs:npairs + nhs]
    acc_s = scratch[npairs + nhs:npairs + 2 * nhs]
    for pp in range(npairs):
        vext_s[pp][meta0:, 0:pair] = v_ref[0, meta0:, pp * pair:(pp + 1) * pair]
        vext_s[pp][meta0:, pair:] = jnp.ones((t_pad - meta0, pair), BF16)
    o_ref[0, 0:seq0, :] = jnp.zeros((seq0, npairs * pair), o_ref.dtype)
    lane = lax.broadcasted_iota(I32, (qb, pair), 1)
    neg = jnp.finfo(F32).min
    heads = range(nhs)

    def scores(hh, qh, qlo, krows):
        return _dot_t(qh[qlo:], k_ref[0, krows, hh * HEAD_PAD:(hh + 1) * HEAD_PAD])

    def absorb(hh, s, qlo, krows, masked):
        if masked:
            rowi = lax.broadcasted_iota(I32, s.shape, 0)
            coli = lax.broadcasted_iota(I32, s.shape, 1)
            s = jnp.where(coli <= rowi, s, neg)
        m_old = m_s[hh][qlo:, :]
        m_new = jnp.maximum(m_old, jnp.max(s, axis=-1, keepdims=True))
        alpha = jnp.exp2(m_old - m_new)
        p = jnp.exp2(s - jnp.concatenate([m_new] * (s.shape[1] // pair), axis=1))
        acc_s[hh][qlo:, :] = (_tile2(alpha) * acc_s[hh][qlo:, :]
                              + _dot(p.astype(BF16), vext_s[hh // 2][krows, :]))
        m_s[hh][qlo:, :] = m_new

    def qblock(i, _):
        q0 = pl.multiple_of(seq0 + i * qb, ATT_BLK)
        qs = [q_ref[0, pl.ds(q0, qb), hh * HEAD_PAD:(hh + 1) * HEAD_PAD] for hh in heads]
        for hh in heads:
            s = _dot_t(qs[hh], k_ref[0, meta0:seq0, hh * HEAD_PAD:(hh + 1) * HEAD_PAD])
            m = jnp.max(s, axis=-1, keepdims=True)
            p = jnp.exp2(s - m)
            acc_s[hh][...] = _dot(p.astype(BF16), vext_s[hh // 2][meta0:seq0, :])
            m_s[hh][...] = jnp.broadcast_to(m, (qb, pair))

        def kv_step(j, _):
            krows = pl.ds(pl.multiple_of(seq0 + j * kb, ATT_BLK), kb)
            for hh in heads:
                absorb(hh, scores(hh, qs[hh], 0, krows), 0, krows, False)
            return 0

        lax.fori_loop(0, i * nsub, kv_step, 0)
        for jj in range(nsub):
            krows = pl.ds(pl.multiple_of(q0 + jj * kb, ATT_BLK), kb)
            for hh in heads:
                absorb(hh, scores(hh, qs[hh], jj * kb, krows), jj * kb, krows, True)
        for pp in range(npairs):
            res = []
            for hh in (2 * pp, 2 * pp + 1):
                acc = acc_s[hh][...]
                res.append(acc[:, :pair] / acc[:, pair:])
            o_ref[0, pl.ds(q0, qb), pp * pair:(pp + 1) * pair] = jnp.where(
                lane < V_HEAD, res[0], res[1]).astype(o_ref.dtype)
        return 0

    lax.fori_loop(0, nq, qblock, 0)


def _b2(q, k, v, *, nh, meta0, seq0, seq, qb, nhs):
    bsz, t_pad, _ = q.shape
    qk = pl.BlockSpec((1, t_pad, nhs * HEAD_PAD), lambda b, p: (b, 0, p))
    vo = pl.BlockSpec((1, t_pad, nhs * V_HEAD), lambda b, p: (b, 0, p))
    return pl.pallas_call(
        functools.partial(_b2_body, meta0=meta0, seq0=seq0, nq=seq // qb, qb=qb, kb=min(ATT_KBLK, qb), nhs=nhs),
        grid=(bsz, nh // nhs),
        in_specs=[qk, qk, vo],
        out_specs=vo,
        out_shape=jax.ShapeDtypeStruct(v.shape, BF16),
        scratch_shapes=([pltpu.VMEM((t_pad, 4 * V_HEAD), BF16)] * (nhs // 2)
                        + [pltpu.VMEM((qb, 2 * V_HEAD), F32)] * nhs
                        + [pltpu.VMEM((qb, 4 * V_HEAD), F32)] * nhs),
        compiler_params=_cparams(("arbitrary", "arbitrary")),
        name="b2_attention",
    )(q, k, v)


B3_TILES = 4


def _b3_body(*refs, tm):
    o_refs = refs[0:2 * B3_TILES:2]
    h_refs = refs[1:2 * B3_TILES:2]
    wo_ref, post_ref, mpre_ref, rw_ref, rb_ref, h3_ref, xn_ref, route_ref, cnt_ref, cnt_s = refs[2 * B3_TILES:]
    tiles = range(B3_TILES)

    @pl.when(pl.program_id(0) == 0)
    def _():
        cnt_s[...] = jnp.zeros_like(cnt_s)

    mix = [_dot(o_refs[t][0], wo_ref[...]) for t in tiles]
    h3 = [h_refs[t][0] + _rms(mix[t], post_ref[...]) for t in tiles]
    xn = [_rms(h3[t], mpre_ref[...]) for t in tiles]
    logits = [_dot(xn[t].astype(BF16), rw_ref[...]) + rb_ref[...] for t in tiles]
    lane = lax.broadcasted_iota(I32, logits[0].shape, 1).astype(F32)
    big = float(LANES)
    count = cnt_s[...]
    for t in tiles:
        rows = slice(t * tm, (t + 1) * tm)
        h3_ref[rows, :] = h3[t]
        xn_ref[rows, :] = xn[t]
        m1 = jnp.max(logits[t], axis=-1, keepdims=True)
        i1 = jnp.min(jnp.where(logits[t] == m1, lane, big), axis=-1, keepdims=True)
        rest = jnp.where(lane == i1, -jnp.inf, logits[t])
        m2 = jnp.max(rest, axis=-1, keepdims=True)
        i2 = jnp.min(jnp.where(rest == m2, lane, big), axis=-1, keepdims=True)
        e2 = jnp.exp(m2 - m1)
        g1 = 1.0 / (1.0 + e2)
        g2 = e2 / (1.0 + e2)
        route_ref[rows, :] = jnp.where(lane == 0.0, g1, jnp.where(lane == 1.0, g2,
                                       jnp.where(lane == 2.0, i1, jnp.where(lane == 3.0, i2, 0.0))))
        onehot = jnp.where((lane == i1) | (lane == i2), 1.0, 0.0)
        count = count + jnp.sum(onehot, axis=0, keepdims=True)
    cnt_s[...] = count
    cnt_ref[...] = jnp.broadcast_to(count, cnt_ref.shape)


def _b3(o, h2, w_o, post, mpre, rw, rb, *, tm, blk0, nper):
    bsz, _, d = h2.shape
    nblk = bsz * nper
    assert nblk % B3_TILES == 0
    src = lambda t: (lambda i: ((B3_TILES * i + t) // nper, blk0 + (B3_TILES * i + t) % nper, 0))
    dst = lambda i: (i, 0)
    fixed = lambda i: (0, 0)
    rows = nblk * tm
    step = B3_TILES * tm
    tile_specs = []
    for t in range(B3_TILES):
        tile_specs += [pl.BlockSpec((1, tm, o.shape[2]), src(t)), pl.BlockSpec((1, tm, d), src(t))]
    return pl.pallas_call(
        functools.partial(_b3_body, tm=tm),
        grid=(nblk // B3_TILES,),
        in_specs=tile_specs + [pl.BlockSpec(w_o.shape, fixed), pl.BlockSpec((1, d), fixed),
                               pl.BlockSpec((1, d), fixed), pl.BlockSpec(rw.shape, fixed),
                               pl.BlockSpec((1, LANES), fixed)],
        out_specs=[pl.BlockSpec((step, d), dst), pl.BlockSpec((step, d), dst), pl.BlockSpec((step, LANES), dst),
                   pl.BlockSpec((8, LANES), fixed)],
        out_shape=[jax.ShapeDtypeStruct((rows, d), F32), jax.ShapeDtypeStruct((rows, d), F32),
                   jax.ShapeDtypeStruct((rows, LANES), F32), jax.ShapeDtypeStruct((8, LANES), F32)],
        scratch_shapes=[pltpu.VMEM((1, LANES), F32)],
        compiler_params=_cparams(("arbitrary",)),
        name="b3_oproj_router",
    )(*([o, h2] * B3_TILES), w_o, post, mpre, rw, rb)


B4_TILES = 4


def _b4_body(route_ref, cnt_ref, meta_ref, run_s, off_s, *, tile, tm):
    lane = lax.broadcasted_iota(I32, (tm, LANES), 1).astype(F32)

    @pl.when(pl.program_id(0) == 0)
    def _():
        total = cnt_ref[0:1, :]
        ntiles = jnp.floor((total + (tile - 1)) * (1.0 / tile))
        rr = lax.broadcasted_iota(I32, (LANES, LANES), 0)
        cc = lax.broadcasted_iota(I32, (LANES, LANES), 1)
        upper = jnp.where(rr < cc, 1.0, 0.0).astype(BF16)
        before = _dot(jnp.broadcast_to(ntiles, (8, LANES)).astype(BF16), upper)
        off_s[...] = before[0:1, :] * float(tile)
        run_s[...] = jnp.zeros_like(run_s)

    rr = lax.broadcasted_iota(I32, (tm, tm), 0)
    cc = lax.broadcasted_iota(I32, (tm, tm), 1)
    lower = jnp.where(cc < rr, 1.0, 0.0).astype(BF16)
    base = run_s[...] + off_s[...]
    for t in range(B4_TILES):
        route = route_ref[t * tm:(t + 1) * tm, :]
        i1 = jnp.sum(jnp.where(lane == 2.0, route, 0.0), axis=-1, keepdims=True)
        i2 = jnp.sum(jnp.where(lane == 3.0, route, 0.0), axis=-1, keepdims=True)
        oh1 = lane == i1
        oh2 = lane == i2
        onehot = jnp.where(oh1 | oh2, 1.0, 0.0)
        pos = _dot(lower, onehot.astype(BF16)) + base
        p1 = jnp.sum(jnp.where(oh1, pos, 0.0), axis=-1, keepdims=True)
        p2 = jnp.sum(jnp.where(oh2, pos, 0.0), axis=-1, keepdims=True)
        packed = jnp.where(lane == 0.0, p1, jnp.where(lane == 1.0, p2, 0.0))
        meta_ref[t] = jnp.transpose(packed)[0:8, :].astype(I32)
        base = base + jnp.sum(onehot, axis=0, keepdims=True)
    run_s[...] = base - off_s[...]


def _b4(route, cnt, *, tm, tile):
    nblk = route.shape[0] // tm
    assert nblk % B4_TILES == 0
    return pl.pallas_call(
        functools.partial(_b4_body, tile=tile, tm=tm),
        grid=(nblk // B4_TILES,),
        in_specs=[pl.BlockSpec((B4_TILES * tm, LANES), lambda i: (i, 0)),
                  pl.BlockSpec((8, LANES), lambda i: (0, 0))],
        out_specs=pl.BlockSpec((B4_TILES, 8, tm), lambda i: (i, 0, 0)),
        out_shape=jax.ShapeDtypeStruct((nblk, 8, tm), I32),
        scratch_shapes=[pltpu.VMEM((1, LANES), F32), pltpu.VMEM((1, LANES), F32)],
        compiler_params=_cparams(("arbitrary",)),
        name="b4_route_rank",
    )(route, cnt)


def _row(ref, pos):
    return ref.at[pos >> 3, pl.ds(pos & (SUBLANES - 1), 1)]


DISPATCH_BUFS = 3


def _b5_body(meta_ref, xn_ref, xg_in_ref, xg_ref, buf_s, in_sem, out_sem, *, tm):
    del xg_in_ref
    i = pl.program_id(0)
    n = pl.num_programs(0)
    groups = tm // SUBLANES
    slot = i % DISPATCH_BUFS
    nxt = (i + 1) % DISPATCH_BUFS

    def fetch(t, s):
        return pltpu.make_async_copy(xn_ref.at[pl.ds(t * groups, groups)], buf_s.at[s], in_sem.at[s])

    def drain(s):
        for k in range(TOP_K):
            pltpu.make_async_copy(buf_s.at[s], xg_ref.at[pl.ds(0, groups)], out_sem.at[s]).wait()

    @pl.when(i == 0)
    def _():
        fetch(0, 0).start()

    @pl.when(i >= DISPATCH_BUFS - 1)
    def _():
        drain(nxt)

    @pl.when(i + 1 < n)
    def _():
        fetch(i + 1, nxt).start()

    fetch(i, slot).wait()

    def issue(g, _):
        for u in range(SUBLANES):
            for k in range(TOP_K):
                pos = meta_ref[k * tm + g * SUBLANES + u]
                pltpu.make_async_copy(buf_s.at[slot, g, pl.ds(u, 1)], _row(xg_ref, pos), out_sem.at[slot]).start()
        return 0

    lax.fori_loop(0, groups, issue, 0)

    @pl.when(i == n - 1)
    def _():
        for back in range(DISPATCH_BUFS - 1):
            @pl.when(i >= back)
            def _():
                drain((i - back) % DISPATCH_BUFS)


def _b5(meta, xn, xg0, *, tm):
    nblk = meta.shape[0] // (TOP_K * tm)
    return pl.pallas_call(
        functools.partial(_b5_body, tm=tm),
        grid=(nblk,),
        in_specs=[pl.BlockSpec((TOP_K * tm,), lambda i: (i,), memory_space=pltpu.SMEM),
                  pl.BlockSpec(memory_space=pl.ANY),
                  pl.BlockSpec(memory_space=pl.ANY)],
        out_specs=pl.BlockSpec(memory_space=pl.ANY),
        scratch_shapes=[pltpu.VMEM((DISPATCH_BUFS, tm // SUBLANES) + xn.shape[1:], xn.dtype),
                        pltpu.SemaphoreType.DMA((DISPATCH_BUFS,)),
                        pltpu.SemaphoreType.DMA((DISPATCH_BUFS,))],
        out_shape=jax.ShapeDtypeStruct(xg0.shape, xg0.dtype),
        input_output_aliases={2: 0},
        compiler_params=_cparams(("arbitrary",)),
        name="b5_dispatch",
    )(meta, xn, xg0)


def _b6_body(te_ref, tv_ref, x_ref, wg_ref, wu_ref, wd_ref, y_ref, xb_s, acc_s):
    j = pl.program_id(0)
    f = pl.program_id(1)

    @pl.when(f == 0)
    def _():
        xb_s[...] = x_ref[...].astype(BF16)
        acc_s[...] = jnp.zeros_like(acc_s)

    @pl.when(tv_ref[j] > 0)
    def _():
        xb = xb_s[...]
        g = _dot(xb, wg_ref[0])
        u = _dot(xb, wu_ref[0])
        acc_s[...] += _dot((g * _sigmoid(g) * u).astype(BF16), wd_ref[0])

    @pl.when(f == pl.num_programs(1) - 1)
    def _():
        y_ref[...] = acc_s[...]


def _b6(te, tv, xg, w_gu, w_d, *, tm, tf):
    rg, d = xg.shape
    eff = w_d.shape[1]
    nf = eff // tf
    nt = rg // tm
    return pl.pallas_call(
        _b6_body,
        grid_spec=pltpu.PrefetchScalarGridSpec(
            num_scalar_prefetch=2,
            grid=(nt, nf),
            in_specs=[pl.BlockSpec((tm, d), lambda j, f, te, tv: (j, 0)),
                      pl.BlockSpec((1, d, tf), lambda j, f, te, tv: (te[j], 0, f)),
                      pl.BlockSpec((1, d, tf), lambda j, f, te, tv: (te[j], 0, nf + f)),
                      pl.BlockSpec((1, tf, d), lambda j, f, te, tv: (te[j], f, 0))],
            out_specs=pl.BlockSpec((tm, d), lambda j, f, te, tv: (j, 0)),
            scratch_shapes=[pltpu.VMEM((tm, d), BF16), pltpu.VMEM((tm, d), F32)]),
        out_shape=jax.ShapeDtypeStruct((rg, d), F32),
        compiler_params=_cparams(("arbitrary", "arbitrary")),
        name="b6_moe_ffn",
    )(te, tv, xg, w_gu, w_gu, w_d)


def _b7_body(meta_ref, meta_next_ref, route_ref, h_ref, post_ref, yg_ref, out_ref, buf_s, sem, *, tm):
    i = pl.program_id(0)
    slot = i % 2

    def gather_tile(mref, s):
        def issue(g, _):
            for u in range(SUBLANES):
                for k in range(TOP_K):
                    pos = mref[k * tm + g * SUBLANES + u]
                    pltpu.make_async_copy(_row(yg_ref, pos), buf_s.at[s, k, g, pl.ds(u, 1)], sem.at[s]).start()
            return 0

        lax.fori_loop(0, tm // SUBLANES, issue, 0)

    @pl.when(i == 0)
    def _():
        gather_tile(meta_ref, 0)

    @pl.when(i + 1 < pl.num_programs(0))
    def _():
        gather_tile(meta_next_ref, 1 - slot)

    for k in range(TOP_K):
        pltpu.make_async_copy(yg_ref.at[pl.ds(0, tm // SUBLANES)], buf_s.at[slot, k], sem.at[slot]).wait()
    route = route_ref[...]
    d = out_ref.shape[1]
    y = route[:, 0:1] * buf_s[slot, 0].reshape(tm, d) + route[:, 1:2] * buf_s[slot, 1].reshape(tm, d)
    out_ref[...] = h_ref[...] + _rms(y, post_ref[...])


def _b7(meta, route, h3, post, yg, *, tm):
    nblk = meta.shape[0] // (TOP_K * tm)
    d = h3.shape[1]
    row = lambda i: (i, 0)
    return pl.pallas_call(
        functools.partial(_b7_body, tm=tm),
        grid=(nblk,),
        in_specs=[pl.BlockSpec((TOP_K * tm,), lambda i: (i,), memory_space=pltpu.SMEM),
                  pl.BlockSpec((TOP_K * tm,), lambda i: (jnp.minimum(i + 1, nblk - 1),),
                               memory_space=pltpu.SMEM),
                  pl.BlockSpec((tm, LANES), row), pl.BlockSpec((tm, d), row),
                  pl.BlockSpec((1, d), lambda i: (0, 0)),
                  pl.BlockSpec(memory_space=pl.ANY)],
        out_specs=pl.BlockSpec((tm, d), row),
        scratch_shapes=[pltpu.VMEM((2, TOP_K, tm // SUBLANES, SUBLANES, d), F32),
                        pltpu.SemaphoreType.DMA((2,))],
        out_shape=jax.ShapeDtypeStruct((nblk * tm, d), F32),
        compiler_params=_cparams(("arbitrary",)),
        name="b7_combine",
    )(meta, meta, route, h3, post, yg)


def _pick(n, target, mult):
    best = mult
    for cand in range(mult, min(n, target) + 1, mult):
        if n % cand == 0:
            best = cand
    return best


def kernel(x, meta_tokens, a_pre_norm, a_w_in, a_conv_w, a_conv_b, a_gate_a_w, a_gate_a_b, a_gate_x_w, a_gate_x_b, a_lambda, a_w_out, a_post_norm, kv_norm, kv_w_dkv, kv_latent_norm, kv_w_ukv, b_pre_norm, b_w_dq, b_q_norm, b_w_uq, b_w_o, b_post_norm, f_pre_norm, f_w_gate_up, f_w_down, f_post_norm, m_pre_norm, m_router_w, m_router_b, m_w_gate_up, m_w_down, m_post_norm):
    bsz, seq, d = x.shape
    n_meta = meta_tokens.shape[0]
    assert a_pre_norm.shape[0] == 1 and b_pre_norm.shape[0] == 1, "one RG-LRU layer + one MLA layer"
    t_all = n_meta + seq
    assert t_all % TT == 0 and seq % ATT_BLK == 0 and bsz % 8 == 0 and n_meta % 16 == 0
    rows = t_all * bsz
    row = lambda v: v.reshape(1, -1).astype(F32)
    bf = lambda w: w.astype(BF16)

    h0 = jnp.concatenate([jnp.broadcast_to(meta_tokens.astype(F32)[:, None, :], (n_meta, bsz, d)),
                          jnp.transpose(x, (1, 0, 2))], axis=0).reshape(rows, d)

    wg = bf(jnp.concatenate([a_gate_a_w[0], a_gate_x_w[0]], axis=-1))
    h1 = _a2(h0, row(a_pre_norm[0]), bf(a_w_in[0]), a_conv_w[0].astype(F32), row(a_conv_b[0]), wg,
             row(a_gate_a_b[0]), row(a_gate_x_b[0]), row(a_lambda[0]), bf(a_w_out[0]), row(a_post_norm[0]),
             tt=TT_SMALL, bsz=bsz)
    d_ff = f_w_down.shape[1]
    h2 = _a3(h1, row(f_pre_norm[0]), bf(f_w_gate_up[0]), bf(f_w_down[0]), row(f_post_norm[0]),
             bsz=bsz, tt=TT, npad=PAD_T // TT, tf=_pick(d_ff, 1024, LANES))
    t_pad = PAD_T + t_all
    seq0 = PAD_T + n_meta
    assert seq0 % ATT_BLK == 0 and t_pad % (TT_SMALL * bsz) == 0

    kv_lora = kv_latent_norm.shape[0]
    nh = b_w_uq.shape[2] // (QK_NOPE + QK_ROPE)
    hr = QK_ROPE // 2
    scale = (QK_NOPE + QK_ROPE) ** -0.5 * math.log2(math.e)
    inv = ROPE_THETA ** (-jnp.arange(0, QK_ROPE, 2, dtype=F32) / QK_ROPE)
    pos = jnp.maximum(jnp.arange(t_pad, dtype=F32) - PAD_T, 0.0)
    ang = pos[:, None] * inv[None, :]
    cos, sin = jnp.cos(ang), jnp.sin(ang)
    one = jnp.ones((t_pad, QK_NOPE), F32)
    zero = jnp.zeros((t_pad, QK_NOPE), F32)
    tq = scale * jnp.concatenate([one, cos, cos, -sin, sin], axis=1)
    tkc = jnp.concatenate([zero, cos, cos, cos, cos], axis=1)
    tks = jnp.concatenate([zero, -sin, sin, -sin, sin], axis=1)

    w_c = kv_w_dkv[:, :kv_lora]
    k1 = kv_w_dkv[:, kv_lora:kv_lora + hr]
    k2 = kv_w_dkv[:, kv_lora + hr:]
    zk = jnp.zeros((d, QK_NOPE), F32)
    wka = jnp.concatenate([zk, k1, k2, k1, k2], axis=1)
    wkb = jnp.concatenate([zk, k2, k1, k2, k1], axis=1)
    ukv = kv_w_ukv.reshape(kv_lora, nh, QK_NOPE + V_HEAD)
    wk = jnp.concatenate([ukv[:, :, :QK_NOPE], jnp.zeros((kv_lora, nh, HEAD_PAD - QK_NOPE), F32)],
                         axis=2).reshape(kv_lora, nh * HEAD_PAD)
    wv = ukv[:, :, QK_NOPE:].reshape(kv_lora, nh * V_HEAD)
    uq = b_w_uq[0].reshape(-1, nh, QK_NOPE + QK_ROPE)
    qn_, q1, q2 = uq[:, :, :QK_NOPE], uq[:, :, QK_NOPE:QK_NOPE + hr], uq[:, :, QK_NOPE + hr:]
    wq = jnp.concatenate([qn_, q1, q2, q2, q1], axis=2).reshape(-1, nh * HEAD_PAD)
    wkv = bf(jnp.concatenate([w_c, wka, wkb], axis=1))
    q, k, v = _b1(h2.reshape(bsz * t_pad, d), row(kv_norm), row(b_pre_norm[0]), wkv,
                  row(kv_latent_norm), bf(wk), bf(wv), bf(b_w_dq[0]), row(b_q_norm[0]), bf(wq), tq, tkc, tks,
                  tm=TT * bsz, nh=nh)

    o = _b2(q.reshape(bsz, t_pad, nh * HEAD_PAD), k.reshape(bsz, t_pad, nh * HEAD_PAD),
            v.reshape(bsz, t_pad, nh * V_HEAD), nh=nh, meta0=PAD_T, seq0=seq0, seq=seq,
            qb=_pick(seq, ATT_QBLK, ATT_BLK), nhs=ATT_HEADS)

    rows_seq = seq * bsz
    tm1 = ATT_BLK
    assert rows_seq % MOE_TILE == 0
    ne = m_router_w.shape[2]
    rw = bf(jnp.concatenate([m_router_w[0], jnp.zeros((d, LANES - ne), F32)], axis=1))
    rb = jnp.concatenate([m_router_b[0].astype(F32), jnp.full((LANES - ne,), -1e30, F32)]).reshape(1, LANES)
    h3, xn, route, cnt = _b3(o, h2, bf(b_w_o[0]), row(b_post_norm[0]), row(m_pre_norm[0]), rw, rb,
                             tm=tm1, blk0=seq0 // tm1, nper=seq // tm1)

    tmg = MOE_TILE
    meta = _b4(route, cnt, tm=tm1, tile=tmg)[:, :TOP_K, :].reshape(-1)
    counts = cnt[0, :ne].astype(I32)
    padded = ((counts + tmg - 1) // tmg) * tmg
    ends = jnp.cumsum(padded)
    nt = (TOP_K * rows_seq) // tmg + ne
    tile_start = jnp.arange(nt, dtype=I32) * tmg
    tv = (tile_start < ends[-1]).astype(I32)
    last = jnp.maximum(ends[-1] // tmg - 1, 0)
    te_raw = jnp.sum((tile_start[:, None] >= ends[None, :]).astype(I32), axis=1)
    te = jnp.minimum(jnp.where(tv > 0, te_raw, te_raw[last]), ne - 1).astype(I32)

    grp = lambda a: a.reshape(a.shape[0] // SUBLANES, SUBLANES, d)
    xg = _b5(meta, grp(xn), jnp.zeros((nt * tmg // SUBLANES, SUBLANES, d), F32), tm=tm1)
    e_ff = m_w_down.shape[2]
    yg = _b6(te, tv, xg.reshape(nt * tmg, d), bf(m_w_gate_up[0]), bf(m_w_down[0]), tm=tmg,
             tf=_pick(e_ff, MOE_FF_CHUNK, LANES))
    h4 = _b7(meta, route, h3, row(m_post_norm[0]), grp(yg), tm=tm1)
    return h4.reshape(bsz, seq, d)
```

```python
import functools
import math

import jax
import jax.numpy as jnp
from jax import lax
from jax.experimental import pallas as pl
from jax.experimental.pallas import tpu as pltpu

F32 = jnp.float32
BF16 = jnp.bfloat16
I32 = jnp.int32

RMS_EPS = 1e-6
RG_C = 8.0
ROPE_THETA = 10000.0
QK_NOPE = 64
QK_ROPE = 32
V_HEAD = 64
HEAD_PAD = 128
TOP_K = 2
LANES = 128
SUBLANES = 8

TT = 48
TT_SMALL = 24
PAD_T = 240
ATT_BLK = 256
ATT_KBLK = 256
ATT_QBLK = 2048
ATT_HEADS = 4
MOE_TILE = 512
MOE_FF_CHUNK = 1792
VMEM_LIMIT = 56 * 1024 * 1024


def _cparams(sem):
    return pltpu.CompilerParams(dimension_semantics=sem, vmem_limit_bytes=VMEM_LIMIT)


def _rms(x, g):
    ms = jnp.mean(x * x, axis=-1, keepdims=True)
    return x * lax.rsqrt(ms + RMS_EPS) * g


def _sigmoid(x):
    return 0.5 * jnp.tanh(0.5 * x) + 0.5


def _dot(a, b):
    return jnp.dot(a, b, preferred_element_type=F32)


def _dot_t(a, b):
    return lax.dot_general(a, b, (((1,), (1,)), ((), ())), preferred_element_type=F32)


def _a2_body(h_ref, hnext_ref, pre_ref, win_ref, cw_ref, cb_ref, wg_ref, gab_ref, gxb_ref, lam_ref,
             wo_ref, pn_ref, out_ref, yx_s, ext_s, a_s, b_s, carry_s, *, nb, tt, bsz, cw):
    tm = tt * bsz
    tail = (cw - 1) * bsz
    c = ext_s.shape[1]
    i = pl.program_id(0)
    slot = i % 2

    def project(href, s):
        xn = _rms(href[...], pre_ref[...]).astype(BF16)
        yx_s[s, :, 0:c] = _dot(xn, win_ref[:, 0:c]).astype(BF16)
        yx_s[s, :, c:] = _dot(xn, win_ref[:, c:]).astype(BF16)

    @pl.when(i == 0)
    def _():
        ext_s[0:tail, :] = jnp.zeros((tail, c), F32)
        carry_s[...] = jnp.zeros_like(carry_s)
        project(h_ref, 0)

    ext_s[tail:tail + tm, :] = yx_s[slot, :, c:].astype(F32)

    xn_next = _rms(hnext_ref[...], pre_ref[...]).astype(BF16)
    pw = 2 * LANES
    pieces = [(s0, min(s0 + pw, 2 * c)) for s0 in range(0, 2 * c, pw)]

    def project_piece(p):
        lo, hi = pieces[p]
        yx_s[1 - slot, :, lo:hi] = _dot(xn_next, win_ref[:, lo:hi]).astype(BF16)

    xc = cb_ref[...] + ext_s[0:tm, :] * cw_ref[0:1, :]
    for k in range(1, cw):
        xc = xc + ext_s[k * bsz:k * bsz + tm, :] * cw_ref[k:k + 1, :]
    ext_s[0:tail, :] = ext_s[tm:tm + tail, :]

    neg_lam = -lam_ref[...]
    sp = jnp.maximum(neg_lam, 0.0) + jnp.log1p(jnp.exp(-jnp.abs(neg_lam)))
    rate = sp * (-RG_C * math.log2(math.e))
    xcb = xc.astype(BF16)
    per_block = -(-len(pieces) // nb)
    for n in range(nb):
        for p in range(n * per_block, min((n + 1) * per_block, len(pieces))):
            project_piece(p)
        sl = slice(n * LANES, (n + 1) * LANES)
        g = _dot(xcb[:, sl], wg_ref[n])
        r = _sigmoid(g[:, :LANES] + gab_ref[:, sl])
        ig = _sigmoid(g[:, LANES:] + gxb_ref[:, sl])
        a = jnp.exp2(r * rate[:, sl])
        a_s[:, sl] = a
        u = 1.0 - a * a
        root = jnp.where(u > 0.0, u * lax.rsqrt(u), 0.0)
        b_s[:, sl] = root * (ig * xc[:, sl])

    def step(t, h):
        rows = pl.ds(pl.multiple_of(t * bsz, bsz), bsz)
        h = a_s[rows, :] * h + b_s[rows, :]
        b_s[rows, :] = h
        return h

    carry_s[...] = lax.fori_loop(0, tt, step, carry_s[...], unroll=True)

    z = (jax.nn.gelu(yx_s[slot, :, 0:c].astype(F32)) * b_s[...]).astype(BF16)
    mix = _dot(z, wo_ref[...])
    out_ref[...] = h_ref[...] + _rms(mix, pn_ref[...])


def _a2(h0, pre, w_in, conv_w, conv_b, wg, ga_b, gx_b, lam, w_out, post, *, tt, bsz):
    r, d = h0.shape
    c = w_in.shape[1] // 2
    nb = wg.shape[0]
    cw = conv_w.shape[0]
    tm = tt * bsz
    nblk = r // tm
    row = lambda i: (i, 0)
    fixed2 = lambda i: (0, 0)
    fixed3 = lambda i: (0, 0, 0)
    return pl.pallas_call(
        functools.partial(_a2_body, nb=nb, tt=tt, bsz=bsz, cw=cw),
        grid=(nblk,),
        in_specs=[pl.BlockSpec((tm, d), row),
                  pl.BlockSpec((tm, d), lambda i: (jnp.minimum(i + 1, nblk - 1), 0)),
                  pl.BlockSpec((1, d), fixed2),
                  pl.BlockSpec((d, 2 * c), fixed2, pipeline_mode=pl.Buffered(1)),
                  pl.BlockSpec((cw, c), fixed2), pl.BlockSpec((1, c), fixed2),
                  pl.BlockSpec((nb, LANES, 2 * LANES), fixed3),
                  pl.BlockSpec((1, c), fixed2), pl.BlockSpec((1, c), fixed2), pl.BlockSpec((1, c), fixed2),
                  pl.BlockSpec((c, d), fixed2, pipeline_mode=pl.Buffered(1)), pl.BlockSpec((1, d), fixed2)],
        out_specs=pl.BlockSpec((tm, d), row),
        out_shape=jax.ShapeDtypeStruct((r, d), F32),
        scratch_shapes=[pltpu.VMEM((2, tm, 2 * c), BF16),
                        pltpu.VMEM((tm + (cw - 1) * bsz, c), F32), pltpu.VMEM((tm, c), F32),
                        pltpu.VMEM((tm, c), F32), pltpu.VMEM((bsz, c), F32)],
        compiler_params=_cparams(("arbitrary",)),
        name="a2_rglru",
    )(h0, h0, pre, w_in, conv_w, conv_b, wg, ga_b, gx_b, lam, w_out, post)


A3_HALVES = 2


def _a3_body(h_ref, pre_ref, wgu_ref, wd_ref, post_ref, out_ref, tr_s, *, npad, bsz, tt, tf):
    i = pl.program_id(0)

    @pl.when(i < npad)
    def _():
        out_ref[...] = jnp.zeros_like(out_ref)

    @pl.when(i >= npad)
    def _():
        halves = range(A3_HALVES)
        th = tt // A3_HALVES
        rows = [slice(s * th * bsz, (s + 1) * th * bsz) for s in halves]
        h = [h_ref[rows[s], :] for s in halves]
        xn = [_rms(h[s], pre_ref[...]).astype(BF16) for s in halves]
        ff = wd_ref.shape[0]
        acc = [None for _ in halves]
        for c in range(ff // tf):
            g = [_dot(xn[s], wgu_ref[:, c * tf:(c + 1) * tf]) for s in halves]
            u = [_dot(xn[s], wgu_ref[:, ff + c * tf:ff + (c + 1) * tf]) for s in halves]
            for s in halves:
                part = _dot((g[s] * _sigmoid(g[s]) * u[s]).astype(BF16), wd_ref[c * tf:(c + 1) * tf, :])
                acc[s] = part if acc[s] is None else acc[s] + part
        for s in halves:
            res = h[s] + _rms(acc[s], post_ref[...])
            for c in range(tr_s.shape[0]):
                tr_s[c, rows[s], :] = res[:, c * LANES:(c + 1) * LANES]
            for b in range(bsz):
                for c in range(tr_s.shape[0]):
                    out_ref[b, s * th:(s + 1) * th, c * LANES:(c + 1) * LANES] = (
                        tr_s[c, pl.ds(s * th * bsz + b, th, stride=bsz), :])


def _a3(h1, pre, w_gu, w_d, post, *, bsz, tt, npad, tf):
    r, d = h1.shape
    tm = tt * bsz
    nblk = r // tm
    fixed = lambda i: (0, 0)
    resident = lambda a: pl.BlockSpec(a.shape, fixed, pipeline_mode=pl.Buffered(1))
    return pl.pallas_call(
        functools.partial(_a3_body, npad=npad, bsz=bsz, tt=tt, tf=tf),
        grid=(npad + nblk,),
        in_specs=[pl.BlockSpec((tm, d), lambda i: (jnp.maximum(i - npad, 0), 0)), pl.BlockSpec((1, d), fixed),
                  resident(w_gu), resident(w_d), pl.BlockSpec((1, d), fixed)],
        out_specs=pl.BlockSpec((bsz, tt, d), lambda i: (0, i, 0)),
        out_shape=jax.ShapeDtypeStruct((bsz, (npad + nblk) * tt, d), F32),
        scratch_shapes=[pltpu.VMEM((d // LANES, tm, LANES), F32)],
        compiler_params=_cparams(("arbitrary",)),
        name="a3_dense_ffn",
    )(h1, pre, w_gu, w_d, post)


def _b1_body(h_ref, kvn_ref, bpre_ref, wkv_ref, latn_ref, wk_ref, wv_ref,
             wdq_ref, qn_ref, wq_ref, tq_ref, tkc_ref, tks_ref, q_ref, k_ref, v_ref, *, nh):
    h = h_ref[...]
    hn = h * lax.rsqrt(jnp.mean(h * h, axis=-1, keepdims=True) + RMS_EPS)
    ukv = (hn * kvn_ref[...]).astype(BF16)
    uq = (hn * bpre_ref[...]).astype(BF16)

    lat = wkv_ref.shape[1] - 2 * HEAD_PAD
    dkv = _dot(ukv, wkv_ref[...])
    ckv = _rms(dkv[:, :lat], latn_ref[...]).astype(BF16)
    kr = dkv[:, lat:lat + HEAD_PAD] * tkc_ref[...] + dkv[:, lat + HEAD_PAD:] * tks_ref[...]
    kn = _dot(ckv, wk_ref[...])
    for hd in range(nh):
        sl = slice(hd * HEAD_PAD, (hd + 1) * HEAD_PAD)
        k_ref[:, sl] = (kn[:, sl] + kr).astype(BF16)
    v_ref[...] = _dot(ckv, wv_ref[...]).astype(BF16)

    cq = _rms(_dot(uq, wdq_ref[...]), qn_ref[...]).astype(BF16)
    qr = _dot(cq, wq_ref[...])
    tq = tq_ref[...]
    for hd in range(nh):
        sl = slice(hd * HEAD_PAD, (hd + 1) * HEAD_PAD)
        q_ref[:, sl] = (qr[:, sl] * tq).astype(BF16)


def _b1(h2, kvn, bpre, wkv, latn, wk, wv, wdq, qn, wq, tq, tkc, tks, *, tm, nh):
    r, d = h2.shape
    nper = tq.shape[0] // tm
    row = lambda i: (i, 0)
    tab = lambda i: (i % nper, 0)
    fixed = lambda i: (0, 0)
    full = lambda a: pl.BlockSpec(a.shape, fixed)
    return pl.pallas_call(
        functools.partial(_b1_body, nh=nh),
        grid=(r // tm,),
        in_specs=[pl.BlockSpec((tm, d), row), full(kvn), full(bpre), full(wkv),
                  full(latn), full(wk), full(wv), full(wdq), full(qn), full(wq),
                  pl.BlockSpec((tm, HEAD_PAD), tab), pl.BlockSpec((tm, HEAD_PAD), tab),
                  pl.BlockSpec((tm, HEAD_PAD), tab)],
        out_specs=[pl.BlockSpec((tm, nh * HEAD_PAD), row), pl.BlockSpec((tm, nh * HEAD_PAD), row),
                   pl.BlockSpec((tm, nh * V_HEAD), row)],
        out_shape=[jax.ShapeDtypeStruct((r, nh * HEAD_PAD), BF16),
                   jax.ShapeDtypeStruct((r, nh * HEAD_PAD), BF16),
                   jax.ShapeDtypeStruct((r, nh * V_HEAD), BF16)],
        compiler_params=_cparams(("arbitrary",)),
        name="b1_qkv_proj",
    )(h2, kvn, bpre, wkv, latn, wk, wv, wdq, qn, wq, tq, tkc, tks)


def _tile2(x):
    return jnp.concatenate([x, x], axis=1)


def _b2_body(q_ref, k_ref, v_ref, o_ref, *scratch, meta0, seq0, nq, qb, kb, nhs):
    pair = 2 * V_HEAD
    t_pad = v_ref.shape[1]
    nsub = qb // kb
    npairs = nhs // 2
    vext_s = scratch[0:npairs]
    m_s = scratch[npairs:npairs + nhs]
    acc_s = scratch[npairs + nhs:npairs + 2 * nhs]
    for pp in range(npairs):
        vext_s[pp][meta0:, 0:pair] = v_ref[0, meta0:, pp * pair:(pp + 1) * pair]
        vext_s[pp][meta0:, pair:] = jnp.ones((t_pad - meta0, pair), BF16)
    o_ref[0, 0:seq0, :] = jnp.zeros((seq0, npairs * pair), o_ref.dtype)
    lane = lax.broadcasted_iota(I32, (qb, pair), 1)
    neg = jnp.finfo(F32).min
    heads = range(nhs)

    def scores(hh, qh, qlo, krows):
        return _dot_t(qh[qlo:], k_ref[0, krows, hh * HEAD_PAD:(hh + 1) * HEAD_PAD])

    def absorb(hh, s, qlo, krows, masked):
        if masked:
            rowi = lax.broadcasted_iota(I32, s.shape, 0)
            coli = lax.broadcasted_iota(I32, s.shape, 1)
            s = jnp.where(coli <= rowi, s, neg)
        m_old = m_s[hh][qlo:, :]
        m_new = jnp.maximum(m_old, jnp.max(s, axis=-1, keepdims=True))
        alpha = jnp.exp2(m_old - m_new)
        p = jnp.exp2(s - jnp.concatenate([m_new] * (s.shape[1] // pair), axis=1))
        acc_s[hh][qlo:, :] = (_tile2(alpha) * acc_s[hh][qlo:, :]
                              + _dot(p.astype(BF16), vext_s[hh // 2][krows, :]))
        m_s[hh][qlo:, :] = m_new

    def qblock(i, _):
        q0 = pl.multiple_of(seq0 + i * qb, ATT_BLK)
        qs = [q_ref[0, pl.ds(q0, qb), hh * HEAD_PAD:(hh + 1) * HEAD_PAD] for hh in heads]
        for hh in heads:
            s = _dot_t(qs[hh], k_ref[0, meta0:seq0, hh * HEAD_PAD:(hh + 1) * HEAD_PAD])
            m = jnp.max(s, axis=-1, keepdims=True)
            p = jnp.exp2(s - m)
            acc_s[hh][...] = _dot(p.astype(BF16), vext_s[hh // 2][meta0:seq0, :])
            m_s[hh][...] = jnp.broadcast_to(m, (qb, pair))

        def kv_step(j, _):
            krows = pl.ds(pl.multiple_of(seq0 + j * kb, ATT_BLK), kb)
            for hh in heads:
                absorb(hh, scores(hh, qs[hh], 0, krows), 0, krows, False)
            return 0

        lax.fori_loop(0, i * nsub, kv_step, 0)
        for jj in range(nsub):
            krows = pl.ds(pl.multiple_of(q0 + jj * kb, ATT_BLK), kb)
            for hh in heads:
                absorb(hh, scores(hh, qs[hh], jj * kb, krows), jj * kb, krows, True)
        for pp in range(npairs):
            res = []
            for hh in (2 * pp, 2 * pp + 1):
                acc = acc_s[hh][...]
                res.append(acc[:, :pair] / acc[:, pair:])
            o_ref[0, pl.ds(q0, qb), pp * pair:(pp + 1) * pair] = jnp.where(
                lane < V_HEAD, res[0], res[1]).astype(o_ref.dtype)
        return 0

    lax.fori_loop(0, nq, qblock, 0)


def _b2(q, k, v, *, nh, meta0, seq0, seq, qb, nhs):
    bsz, t_pad, _ = q.shape
    qk = pl.BlockSpec((1, t_pad, nhs * HEAD_PAD), lambda b, p: (b, 0, p))
    vo = pl.BlockSpec((1, t_pad, nhs * V_HEAD), lambda b, p: (b, 0, p))
    return pl.pallas_call(
        functools.partial(_b2_body, meta0=meta0, seq0=seq0, nq=seq // qb, qb=qb, kb=min(ATT_KBLK, qb), nhs=nhs),
        grid=(bsz, nh // nhs),
        in_specs=[qk, qk, vo],
        out_specs=vo,
        out_shape=jax.ShapeDtypeStruct(v.shape, BF16),
        scratch_shapes=([pltpu.VMEM((t_pad, 4 * V_HEAD), BF16)] * (nhs // 2)
                        + [pltpu.VMEM((qb, 2 * V_HEAD), F32)] * nhs
                        + [pltpu.VMEM((qb, 4 * V_HEAD), F32)] * nhs),
        compiler_params=_cparams(("arbitrary", "arbitrary")),
        name="b2_attention",
    )(q, k, v)


B3_TILES = 4


def _b3_body(*refs, tm):
    o_refs = refs[0:2 * B3_TILES:2]
    h_refs = refs[1:2 * B3_TILES:2]
    wo_ref, post_ref, mpre_ref, rw_ref, rb_ref, h3_ref, xn_ref, route_ref, cnt_ref, cnt_s = refs[2 * B3_TILES:]
    tiles = range(B3_TILES)

    @pl.when(pl.program_id(0) == 0)
    def _():
        cnt_s[...] = jnp.zeros_like(cnt_s)

    mix = [_dot(o_refs[t][0], wo_ref[...]) for t in tiles]
    h3 = [h_refs[t][0] + _rms(mix[t], post_ref[...]) for t in tiles]
    xn = [_rms(h3[t], mpre_ref[...]) for t in tiles]
    logits = [_dot(xn[t].astype(BF16), rw_ref[...]) + rb_ref[...] for t in tiles]
    lane = lax.broadcasted_iota(I32, logits[0].shape, 1).astype(F32)
    big = float(LANES)
    count = cnt_s[...]
    for t in tiles:
        rows = slice(t * tm, (t + 1) * tm)
        h3_ref[rows, :] = h3[t]
        xn_ref[rows, :] = xn[t]
        m1 = jnp.max(logits[t], axis=-1, keepdims=True)
        i1 = jnp.min(jnp.where(logits[t] == m1, lane, big), axis=-1, keepdims=True)
        rest = jnp.where(lane == i1, -jnp.inf, logits[t])
        m2 = jnp.max(rest, axis=-1, keepdims=True)
        i2 = jnp.min(jnp.where(rest == m2, lane, big), axis=-1, keepdims=True)
        e2 = jnp.exp(m2 - m1)
        g1 = 1.0 / (1.0 + e2)
        g2 = e2 / (1.0 + e2)
        route_ref[rows, :] = jnp.where(lane == 0.0, g1, jnp.where(lane == 1.0, g2,
                                       jnp.where(lane == 2.0, i1, jnp.where(lane == 3.0, i2, 0.0))))
        onehot = jnp.where((lane == i1) | (lane == i2), 1.0, 0.0)
        count = count + jnp.sum(onehot, axis=0, keepdims=True)
    cnt_s[...] = count
    cnt_ref[...] = jnp.broadcast_to(count, cnt_ref.shape)


def _b3(o, h2, w_o, post, mpre, rw, rb, *, tm, blk0, nper):
    bsz, _, d = h2.shape
    nblk = bsz * nper
    assert nblk % B3_TILES == 0
    src = lambda t: (lambda i: ((B3_TILES * i + t) // nper, blk0 + (B3_TILES * i + t) % nper, 0))
    dst = lambda i: (i, 0)
    fixed = lambda i: (0, 0)
    rows = nblk * tm
    step = B3_TILES * tm
    tile_specs = []
    for t in range(B3_TILES):
        tile_specs += [pl.BlockSpec((1, tm, o.shape[2]), src(t)), pl.BlockSpec((1, tm, d), src(t))]
    return pl.pallas_call(
        functools.partial(_b3_body, tm=tm),
        grid=(nblk // B3_TILES,),
        in_specs=tile_specs + [pl.BlockSpec(w_o.shape, fixed), pl.BlockSpec((1, d), fixed),
                               pl.BlockSpec((1, d), fixed), pl.BlockSpec(rw.shape, fixed),
                               pl.BlockSpec((1, LANES), fixed)],
        out_specs=[pl.BlockSpec((step, d), dst), pl.BlockSpec((step, d), dst), pl.BlockSpec((step, LANES), dst),
                   pl.BlockSpec((8, LANES), fixed)],
        out_shape=[jax.ShapeDtypeStruct((rows, d), F32), jax.ShapeDtypeStruct((rows, d), F32),
                   jax.ShapeDtypeStruct((rows, LANES), F32), jax.ShapeDtypeStruct((8, LANES), F32)],
        scratch_shapes=[pltpu.VMEM((1, LANES), F32)],
        compiler_params=_cparams(("arbitrary",)),
        name="b3_oproj_router",
    )(*([o, h2] * B3_TILES), w_o, post, mpre, rw, rb)


B4_TILES = 4


def _b4_body(route_ref, cnt_ref, meta_ref, run_s, off_s, *, tile, tm):
    lane = lax.broadcasted_iota(I32, (tm, LANES), 1).astype(F32)

    @pl.when(pl.program_id(0) == 0)
    def _():
        total = cnt_ref[0:1, :]
        ntiles = jnp.floor((total + (tile - 1)) * (1.0 / tile))
        rr = lax.broadcasted_iota(I32, (LANES, LANES), 0)
        cc = lax.broadcasted_iota(I32, (LANES, LANES), 1)
        upper = jnp.where(rr < cc, 1.0, 0.0).astype(BF16)
        before = _dot(jnp.broadcast_to(ntiles, (8, LANES)).astype(BF16), upper)
        off_s[...] = before[0:1, :] * float(tile)
        run_s[...] = jnp.zeros_like(run_s)

    rr = lax.broadcasted_iota(I32, (tm, tm), 0)
    cc = lax.broadcasted_iota(I32, (tm, tm), 1)
    lower = jnp.where(cc < rr, 1.0, 0.0).astype(BF16)
    base = run_s[...] + off_s[...]
    for t in range(B4_TILES):
        route = route_ref[t * tm:(t + 1) * tm, :]
        i1 = jnp.sum(jnp.where(lane == 2.0, route, 0.0), axis=-1, keepdims=True)
        i2 = jnp.sum(jnp.where(lane == 3.0, route, 0.0), axis=-1, keepdims=True)
        oh1 = lane == i1
        oh2 = lane == i2
        onehot = jnp.where(oh1 | oh2, 1.0, 0.0)
        pos = _dot(lower, onehot.astype(BF16)) + base
        p1 = jnp.sum(jnp.where(oh1, pos, 0.0), axis=-1, keepdims=True)
        p2 = jnp.sum(jnp.where(oh2, pos, 0.0), axis=-1, keepdims=True)
        packed = jnp.where(lane == 0.0, p1, jnp.where(lane == 1.0, p2, 0.0))
        meta_ref[t] = jnp.transpose(packed)[0:8, :].astype(I32)
        base = base + jnp.sum(onehot, axis=0, keepdims=True)
    run_s[...] = base - off_s[...]


def _b4(route, cnt, *, tm, tile):
    nblk = route.shape[0] // tm
    assert nblk % B4_TILES == 0
    return pl.pallas_call(
        functools.partial(_b4_body, tile=tile, tm=tm),
        grid=(nblk // B4_TILES,),
        in_specs=[pl.BlockSpec((B4_TILES * tm, LANES), lambda i: (i, 0)),
                  pl.BlockSpec((8, LANES), lambda i: (0, 0))],
        out_specs=pl.BlockSpec((B4_TILES, 8, tm), lambda i: (i, 0, 0)),
        out_shape=jax.ShapeDtypeStruct((nblk, 8, tm), I32),
        scratch_shapes=[pltpu.VMEM((1, LANES), F32), pltpu.VMEM((1, LANES), F32)],
        compiler_params=_cparams(("arbitrary",)),
        name="b4_route_rank",
    )(route, cnt)


def _row(ref, pos):
    return ref.at[pos >> 3, pl.ds(pos & (SUBLANES - 1), 1)]


DISPATCH_BUFS = 3


def _b5_body(meta_ref, xn_ref, xg_in_ref, xg_ref, buf_s, in_sem, out_sem, *, tm):
    del xg_in_ref
    i = pl.program_id(0)
    n = pl.num_programs(0)
    groups = tm // SUBLANES
    slot = i % DISPATCH_BUFS
    nxt = (i + 1) % DISPATCH_BUFS

    def fetch(t, s):
        return pltpu.make_async_copy(xn_ref.at[pl.ds(t * groups, groups)], buf_s.at[s], in_sem.at[s])

    def drain(s):
        for k in range(TOP_K):
            pltpu.make_async_copy(buf_s.at[s], xg_ref.at[pl.ds(0, groups)], out_sem.at[s]).wait()

    @pl.when(i == 0)
    def _():
        fetch(0, 0).start()

    @pl.when(i >= DISPATCH_BUFS - 1)
    def _():
        drain(nxt)

    @pl.when(i + 1 < n)
    def _():
        fetch(i + 1, nxt).start()

    fetch(i, slot).wait()

    def issue(g, _):
        for u in range(SUBLANES):
            for k in range(TOP_K):
                pos = meta_ref[k * tm + g * SUBLANES + u]
                pltpu.make_async_copy(buf_s.at[slot, g, pl.ds(u, 1)], _row(xg_ref, pos),
                                      out_sem.at[slot]).start(priority=k)
        return 0

    lax.fori_loop(0, groups, issue, 0)

    @pl.when(i == n - 1)
    def _():
        for back in range(DISPATCH_BUFS - 1):
            @pl.when(i >= back)
            def _():
                drain((i - back) % DISPATCH_BUFS)


def _b5(meta, xn, xg0, *, tm):
    nblk = meta.shape[0] // (TOP_K * tm)
    return pl.pallas_call(
        functools.partial(_b5_body, tm=tm),
        grid=(nblk,),
        in_specs=[pl.BlockSpec((TOP_K * tm,), lambda i: (i,), memory_space=pltpu.SMEM),
                  pl.BlockSpec(memory_space=pl.ANY),
                  pl.BlockSpec(memory_space=pl.ANY)],
        out_specs=pl.BlockSpec(memory_space=pl.ANY),
        scratch_shapes=[pltpu.VMEM((DISPATCH_BUFS, tm // SUBLANES) + xn.shape[1:], xn.dtype),
                        pltpu.SemaphoreType.DMA((DISPATCH_BUFS,)),
                        pltpu.SemaphoreType.DMA((DISPATCH_BUFS,))],
        out_shape=jax.ShapeDtypeStruct(xg0.shape, xg0.dtype),
        input_output_aliases={2: 0},
        compiler_params=_cparams(("arbitrary",)),
        name="b5_dispatch",
    )(meta, xn, xg0)


def _b6_body(te_ref, tv_ref, x_ref, wg_ref, wu_ref, wd_ref, y_ref, xb_s, acc_s):
    j = pl.program_id(0)
    f = pl.program_id(1)

    @pl.when(f == 0)
    def _():
        xb_s[...] = x_ref[...].astype(BF16)
        acc_s[...] = jnp.zeros_like(acc_s)

    @pl.when(tv_ref[j] > 0)
    def _():
        xb = xb_s[...]
        g = _dot(xb, wg_ref[0])
        u = _dot(xb, wu_ref[0])
        acc_s[...] += _dot((g * _sigmoid(g) * u).astype(BF16), wd_ref[0])

    @pl.when(f == pl.num_programs(1) - 1)
    def _():
        y_ref[...] = acc_s[...]


def _b6(te, tv, xg, w_gu, w_d, *, tm, tf):
    rg, d = xg.shape
    eff = w_d.shape[1]
    nf = eff // tf
    nt = rg // tm
    return pl.pallas_call(
        _b6_body,
        grid_spec=pltpu.PrefetchScalarGridSpec(
            num_scalar_prefetch=2,
            grid=(nt, nf),
            in_specs=[pl.BlockSpec((tm, d), lambda j, f, te, tv: (j, 0)),
                      pl.BlockSpec((1, d, tf), lambda j, f, te, tv: (te[j], 0, f)),
                      pl.BlockSpec((1, d, tf), lambda j, f, te, tv: (te[j], 0, nf + f)),
                      pl.BlockSpec((1, tf, d), lambda j, f, te, tv: (te[j], f, 0))],
            out_specs=pl.BlockSpec((tm, d), lambda j, f, te, tv: (j, 0)),
            scratch_shapes=[pltpu.VMEM((tm, d), BF16), pltpu.VMEM((tm, d), F32)]),
        out_shape=jax.ShapeDtypeStruct((rg, d), F32),
        compiler_params=_cparams(("arbitrary", "arbitrary")),
        name="b6_moe_ffn",
    )(te, tv, xg, w_gu, w_gu, w_d)


def _b7_body(meta_ref, meta_next_ref, route_ref, h_ref, post_ref, yg_ref, out_ref, buf_s, sem, *, tm):
    i = pl.program_id(0)
    slot = i % 2

    def gather_tile(mref, s):
        def issue(g, _):
            for u in range(SUBLANES):
                for k in range(TOP_K):
                    pos = mref[k * tm + g * SUBLANES + u]
                    pltpu.make_async_copy(_row(yg_ref, pos), buf_s.at[s, k, g, pl.ds(u, 1)],
                                          sem.at[s]).start(priority=k)
            return 0

        lax.fori_loop(0, tm // SUBLANES, issue, 0)

    @pl.when(i == 0)
    def _():
        gather_tile(meta_ref, 0)

    @pl.when(i + 1 < pl.num_programs(0))
    def _():
        gather_tile(meta_next_ref, 1 - slot)

    for k in range(TOP_K):
        pltpu.make_async_copy(yg_ref.at[pl.ds(0, tm // SUBLANES)], buf_s.at[slot, k], sem.at[slot]).wait()
    route = route_ref[...]
    d = out_ref.shape[1]
    y = route[:, 0:1] * buf_s[slot, 0].reshape(tm, d) + route[:, 1:2] * buf_s[slot, 1].reshape(tm, d)
    out_ref[...] = h_ref[...] + _rms(y, post_ref[...])


def _b7(meta, route, h3, post, yg, *, tm):
    nblk = meta.shape[0] // (TOP_K * tm)
    d = h3.shape[1]
    row = lambda i: (i, 0)
    return pl.pallas_call(
        functools.partial(_b7_body, tm=tm),
        grid=(nblk,),
        in_specs=[pl.BlockSpec((TOP_K * tm,), lambda i: (i,), memory_space=pltpu.SMEM),
                  pl.BlockSpec((TOP_K * tm,), lambda i: (jnp.minimum(i + 1, nblk - 1),),
                               memory_space=pltpu.SMEM),
                  pl.BlockSpec((tm, LANES), row), pl.BlockSpec((tm, d), row),
                  pl.BlockSpec((1, d), lambda i: (0, 0)),
                  pl.BlockSpec(memory_space=pl.ANY)],
        out_specs=pl.BlockSpec((tm, d), row),
        scratch_shapes=[pltpu.VMEM((2, TOP_K, tm // SUBLANES, SUBLANES, d), F32),
                        pltpu.SemaphoreType.DMA((2,))],
        out_shape=jax.ShapeDtypeStruct((nblk * tm, d), F32),
        compiler_params=_cparams(("arbitrary",)),
        name="b7_combine",
    )(meta, meta, route, h3, post, yg)


def _pick(n, target, mult):
    best = mult
    for cand in range(mult, min(n, target) + 1, mult):
        if n % cand == 0:
            best = cand
    return best


def kernel(x, meta_tokens, a_pre_norm, a_w_in, a_conv_w, a_conv_b, a_gate_a_w, a_gate_a_b, a_gate_x_w, a_gate_x_b, a_lambda, a_w_out, a_post_norm, kv_norm, kv_w_dkv, kv_latent_norm, kv_w_ukv, b_pre_norm, b_w_dq, b_q_norm, b_w_uq, b_w_o, b_post_norm, f_pre_norm, f_w_gate_up, f_w_down, f_post_norm, m_pre_norm, m_router_w, m_router_b, m_w_gate_up, m_w_down, m_post_norm):
    bsz, seq, d = x.shape
    n_meta = meta_tokens.shape[0]
    assert a_pre_norm.shape[0] == 1 and b_pre_norm.shape[0] == 1, "one RG-LRU layer + one MLA layer"
    t_all = n_meta + seq
    assert t_all % TT == 0 and seq % ATT_BLK == 0 and bsz % 8 == 0 and n_meta % 16 == 0
    rows = t_all * bsz
    row = lambda v: v.reshape(1, -1).astype(F32)
    bf = lambda w: w.astype(BF16)

    h0 = jnp.concatenate([jnp.broadcast_to(meta_tokens.astype(F32)[:, None, :], (n_meta, bsz, d)),
                          jnp.transpose(x, (1, 0, 2))], axis=0).reshape(rows, d)

    wg = bf(jnp.concatenate([a_gate_a_w[0], a_gate_x_w[0]], axis=-1))
    h1 = _a2(h0, row(a_pre_norm[0]), bf(a_w_in[0]), a_conv_w[0].astype(F32), row(a_conv_b[0]), wg,
             row(a_gate_a_b[0]), row(a_gate_x_b[0]), row(a_lambda[0]), bf(a_w_out[0]), row(a_post_norm[0]),
             tt=TT_SMALL, bsz=bsz)
    d_ff = f_w_down.shape[1]
    h2 = _a3(h1, row(f_pre_norm[0]), bf(f_w_gate_up[0]), bf(f_w_down[0]), row(f_post_norm[0]),
             bsz=bsz, tt=TT, npad=PAD_T // TT, tf=_pick(d_ff, 1024, LANES))
    t_pad = PAD_T + t_all
    seq0 = PAD_T + n_meta
    assert seq0 % ATT_BLK == 0 and t_pad % (TT_SMALL * bsz) == 0

    kv_lora = kv_latent_norm.shape[0]
    nh = b_w_uq.shape[2] // (QK_NOPE + QK_ROPE)
    hr = QK_ROPE // 2
    scale = (QK_NOPE + QK_ROPE) ** -0.5 * math.log2(math.e)
    inv = ROPE_THETA ** (-jnp.arange(0, QK_ROPE, 2, dtype=F32) / QK_ROPE)
    pos = jnp.maximum(jnp.arange(t_pad, dtype=F32) - PAD_T, 0.0)
    ang = pos[:, None] * inv[None, :]
    cos, sin = jnp.cos(ang), jnp.sin(ang)
    one = jnp.ones((t_pad, QK_NOPE), F32)
    zero = jnp.zeros((t_pad, QK_NOPE), F32)
    tq = scale * jnp.concatenate([one, cos, cos, -sin, sin], axis=1)
    tkc = jnp.concatenate([zero, cos, cos, cos, cos], axis=1)
    tks = jnp.concatenate([zero, -sin, sin, -sin, sin], axis=1)

    w_c = kv_w_dkv[:, :kv_lora]
    k1 = kv_w_dkv[:, kv_lora:kv_lora + hr]
    k2 = kv_w_dkv[:, kv_lora + hr:]
    zk = jnp.zeros((d, QK_NOPE), F32)
    wka = jnp.concatenate([zk, k1, k2, k1, k2], axis=1)
    wkb = jnp.concatenate([zk, k2, k1, k2, k1], axis=1)
    ukv = kv_w_ukv.reshape(kv_lora, nh, QK_NOPE + V_HEAD)
    wk = jnp.concatenate([ukv[:, :, :QK_NOPE], jnp.zeros((kv_lora, nh, HEAD_PAD - QK_NOPE), F32)],
                         axis=2).reshape(kv_lora, nh * HEAD_PAD)
    wv = ukv[:, :, QK_NOPE:].reshape(kv_lora, nh * V_HEAD)
    uq = b_w_uq[0].reshape(-1, nh, QK_NOPE + QK_ROPE)
    qn_, q1, q2 = uq[:, :, :QK_NOPE], uq[:, :, QK_NOPE:QK_NOPE + hr], uq[:, :, QK_NOPE + hr:]
    wq = jnp.concatenate([qn_, q1, q2, q2, q1], axis=2).reshape(-1, nh * HEAD_PAD)
    wkv = bf(jnp.concatenate([w_c, wka, wkb], axis=1))
    q, k, v = _b1(h2.reshape(bsz * t_pad, d), row(kv_norm), row(b_pre_norm[0]), wkv,
                  row(kv_latent_norm), bf(wk), bf(wv), bf(b_w_dq[0]), row(b_q_norm[0]), bf(wq), tq, tkc, tks,
                  tm=TT * bsz, nh=nh)

    o = _b2(q.reshape(bsz, t_pad, nh * HEAD_PAD), k.reshape(bsz, t_pad, nh * HEAD_PAD),
            v.reshape(bsz, t_pad, nh * V_HEAD), nh=nh, meta0=PAD_T, seq0=seq0, seq=seq,
            qb=_pick(seq, ATT_QBLK, ATT_BLK), nhs=ATT_HEADS)

    rows_seq = seq * bsz
    tm1 = ATT_BLK
    assert rows_seq % MOE_TILE == 0
    ne = m_router_w.shape[2]
    rw = bf(jnp.concatenate([m_router_w[0], jnp.zeros((d, LANES - ne), F32)], axis=1))
    rb = jnp.concatenate([m_router_b[0].astype(F32), jnp.full((LANES - ne,), -1e30, F32)]).reshape(1, LANES)
    h3, xn, route, cnt = _b3(o, h2, bf(b_w_o[0]), row(b_post_norm[0]), row(m_pre_norm[0]), rw, rb,
                             tm=tm1, blk0=seq0 // tm1, nper=seq // tm1)

    tmg = MOE_TILE
    meta = _b4(route, cnt, tm=tm1, tile=tmg)[:, :TOP_K, :].reshape(-1)
    counts = cnt[0, :ne].astype(I32)
    padded = ((counts + tmg - 1) // tmg) * tmg
    ends = jnp.cumsum(padded)
    nt = (TOP_K * rows_seq) // tmg + ne
    tile_start = jnp.arange(nt, dtype=I32) * tmg
    tv = (tile_start < ends[-1]).astype(I32)
    last = jnp.maximum(ends[-1] // tmg - 1, 0)
    te_raw = jnp.sum((tile_start[:, None] >= ends[None, :]).astype(I32), axis=1)
    te = jnp.minimum(jnp.where(tv > 0, te_raw, te_raw[last]), ne - 1).astype(I32)

    grp = lambda a: a.reshape(a.shape[0] // SUBLANES, SUBLANES, d)
    xg = _b5(meta, grp(xn), jnp.zeros((nt * tmg // SUBLANES, SUBLANES, d), F32), tm=tm1)
    e_ff = m_w_down.shape[2]
    yg = _b6(te, tv, xg.reshape(nt * tmg, d), bf(m_w_gate_up[0]), bf(m_w_down[0]), tm=tmg,
             tf=_pick(e_ff, MOE_FF_CHUNK, LANES))
    h4 = _b7(meta, route, h3, row(m_post_norm[0]), grp(yg), tm=tm1)
    return h4.reshape(bsz, seq, d)
```
